```python
import jax, jax.numpy as jnp
from jax import lax
import numpy as np

D_MODEL = 1024
BATCH = 16
SEQ = 2048
DEPTH = 2
DEC_BATCH = 8
DEC_SEQ = 16
PAST_LEN = 1024

CHUNK = 64
CONV_W = 4
EPS = 1e-6
SSD_HEADS = 16
SSD_HEAD_DIM = 64
SSD_INNER = SSD_HEADS * SSD_HEAD_DIM
SSD_GROUPS = 2
SSD_STATE = 64
SSD_CONV_DIM = SSD_INNER + 2 * SSD_GROUPS * SSD_STATE
LRU_WIDTH = 1024
LRU_BLOCKS = 16
LRU_BLOCK_DIM = LRU_WIDTH // LRU_BLOCKS
LRU_C = 8.0
PEER_HEADS = 8
PEER_KEYS = 128
PEER_EXPERTS = PEER_KEYS * PEER_KEYS
PEER_KEY_DIM = 256
PEER_TOPK = 16
PEER_BLOCK = 128
Z_END = SSD_INNER
XBC_END = Z_END + SSD_CONV_DIM
DT_END = XBC_END + SSD_HEADS
LX_END = DT_END + LRU_WIDTH
LY_END = LX_END + LRU_WIDTH
GS_END = LY_END + D_MODEL
IN_COLS = GS_END + D_MODEL

kernel_name = 'hybrid_ssd_rglru_peer_stream_step'


def rmsnorm(x, w):
    xf = x.astype(jnp.float32)
    y = xf * lax.rsqrt(jnp.mean(xf * xf, axis=-1, keepdims=True) + EPS)
    return (y * w.astype(jnp.float32)).astype(x.dtype)


def causal_conv(x, buf, w, b):
    L = x.shape[1]
    xp = jnp.concatenate([buf.astype(x.dtype), x], axis=1)
    y = b + w[0] * xp[:, 0:L]
    for k in range(1, CONV_W):
        y = y + w[k] * xp[:, k:k + L]
    return y, xp[:, -(CONV_W - 1):]


def ssd_scan(x, dt, a, bm, cm, h0):
    f32 = jnp.float32
    bsz, L = x.shape[0], x.shape[1]
    nc = -(-L // CHUNK)
    pad = nc * CHUNK - L
    hg = SSD_HEADS // SSD_GROUPS
    def padl(t):
        return jnp.pad(t.astype(f32), [(0, 0), (0, pad)] + [(0, 0)] * (t.ndim - 2))
    xc = padl(x).reshape(bsz, nc, CHUNK, SSD_GROUPS, hg, SSD_HEAD_DIM)
    dtc = padl(dt).reshape(bsz, nc, CHUNK, SSD_GROUPS, hg)
    bc = padl(bm).reshape(bsz, nc, CHUNK, SSD_GROUPS, SSD_STATE)
    cc = padl(cm).reshape(bsz, nc, CHUNK, SSD_GROUPS, SSD_STATE)
    acum = jnp.cumsum(dtc * a.astype(f32).reshape(SSD_GROUPS, hg), axis=2)
    causal = jnp.tril(jnp.ones((CHUNK, CHUNK), dtype=bool))[None, None, :, :, None, None]
    seg = acum[:, :, :, None] - acum[:, :, None, :]
    decay = jnp.exp(jnp.where(causal, seg, -jnp.inf))
    cb = jnp.einsum('bcqgn,bcsgn->bcqsg', cc, bc)
    wts = cb[..., None] * decay * dtc[:, :, None]
    y_diag = jnp.einsum('bcqsgh,bcsghp->bcqghp', wts, xc)
    decay_end = jnp.exp(acum[:, :, -1:] - acum)
    chunk_states = jnp.einsum('bcsgn,bcsghp->bcghpn', bc, (decay_end * dtc)[..., None] * xc)
    chunk_decay = jnp.exp(acum[:, :, -1])
    h0g = h0.astype(f32).reshape(bsz, SSD_GROUPS, hg, SSD_HEAD_DIM, SSD_STATE)
    def step(h, inp):
        dec, st = inp
        return dec[..., None, None] * h + st, h
    h_fin, h_in = lax.scan(step, h0g, (jnp.moveaxis(chunk_decay, 1, 0), jnp.moveaxis(chunk_states, 1, 0)))
    h_in = jnp.moveaxis(h_in, 0, 1)
    y_off = jnp.einsum('bcqgn,bcghpn->bcqghp', cc, h_in) * jnp.exp(acum)[..., None]
    y = (y_diag + y_off).reshape(bsz, nc * CHUNK, SSD_HEADS, SSD_HEAD_DIM)[:, :L]
    return y, h_fin.reshape(bsz, SSD_HEADS, SSD_HEAD_DIM, SSD_STATE)


def rglru(xb, h0, wa, ba, wx, bx, lam):
    f32 = jnp.float32
    bsz, L = xb.shape[0], xb.shape[1]
    xf = xb.astype(f32)
    xblk = xf.reshape(bsz, L, LRU_BLOCKS, LRU_BLOCK_DIM)
    r = jax.nn.sigmoid(jnp.einsum('blnd,nde->blne', xblk, wa.astype(f32)).reshape(bsz, L, LRU_WIDTH) + ba.astype(f32))
    i = jax.nn.sigmoid(jnp.einsum('blnd,nde->blne', xblk, wx.astype(f32)).reshape(bsz, L, LRU_WIDTH) + bx.astype(f32))
    log_a = -LRU_C * r * jax.nn.softplus(-lam.astype(f32))
    a = jnp.exp(log_a)
    u = jnp.sqrt(-jnp.expm1(2.0 * log_a)) * (i * xf)
    def step(h, inp):
        a_t, u_t = inp
        h = a_t * h + u_t
        return h, h
    h_last, hs = lax.scan(step, h0.astype(f32), (jnp.swapaxes(a, 0, 1), jnp.swapaxes(u, 0, 1)))
    return jnp.swapaxes(hs, 0, 1), h_last


def token_mixer(h, ssd_h0, ssd_buf, lru_h0, lru_buf, w_in, ssd_conv_w, ssd_conv_b, ssd_dt_bias, ssd_a_log,
                ssd_d, ssd_norm_w, lru_conv_w, lru_conv_b, lru_wa, lru_ba, lru_wx, lru_bx, lru_lambda,
                w_br_ssd, w_br_lru, w_out):
    bsz, L = h.shape[0], h.shape[1]
    proj = h @ w_in
    z, xbc, dt, lx, ly, g_ssd, g_lru = jnp.split(proj, [Z_END, XBC_END, DT_END, LX_END, LY_END, GS_END], axis=-1)
    xbc, ssd_buf_new = causal_conv(xbc, ssd_buf, ssd_conv_w, ssd_conv_b)
    xbc = jax.nn.silu(xbc)
    xs, bm, cm = jnp.split(xbc, [SSD_INNER, SSD_INNER + SSD_GROUPS * SSD_STATE], axis=-1)
    xs = xs.reshape(bsz, L, SSD_HEADS, SSD_HEAD_DIM)
    bm = bm.reshape(bsz, L, SSD_GROUPS, SSD_STATE)
    cm = cm.reshape(bsz, L, SSD_GROUPS, SSD_STATE)
    dtf = jax.nn.softplus(dt.astype(jnp.float32) + ssd_dt_bias.astype(jnp.float32))
    a = -jnp.exp(ssd_a_log.astype(jnp.float32))
    y, ssd_h_new = ssd_scan(xs, dtf, a, bm, cm, ssd_h0)
    y = y + ssd_d.astype(jnp.float32)[:, None] * xs.astype(jnp.float32)
    y = y.reshape(bsz, L, SSD_INNER).astype(h.dtype) * jax.nn.silu(z)
    y_ssd = rmsnorm(y, ssd_norm_w)
    lx, lru_buf_new = causal_conv(lx, lru_buf, lru_conv_w, lru_conv_b)
    hs, lru_h_new = rglru(lx, lru_h0, lru_wa, lru_ba, lru_wx, lru_bx, lru_lambda)
    y_lru = hs.astype(h.dtype) * jax.nn.gelu(ly)
    merged = jax.nn.sigmoid(g_ssd) * (y_ssd @ w_br_ssd) + jax.nn.sigmoid(g_lru) * (y_lru @ w_br_lru)
    return merged @ w_out, ssd_h_new, ssd_buf_new, lru_h_new, lru_buf_new


def peer(h, wq, k1, k2, u, v):
    f32 = jnp.float32
    bsz, L, D = h.shape
    T = bsz * L
    hf = h.reshape(T, D)
    q = (hf @ wq).astype(f32).reshape(T, PEER_HEADS, PEER_KEY_DIM)
    half = PEER_KEY_DIM // 2
    s1 = jnp.einsum('thd,kd->thk', q[..., :half], k1.astype(f32))
    s2 = jnp.einsum('thd,kd->thk', q[..., half:], k2.astype(f32))
    v1, i1 = lax.top_k(s1, PEER_TOPK)
    v2, i2 = lax.top_k(s2, PEER_TOPK)
    cand = (v1[..., :, None] + v2[..., None, :]).reshape(T, PEER_HEADS, PEER_TOPK * PEER_TOPK)
    sv, si = lax.top_k(cand, PEER_TOPK)
    e = (jnp.take_along_axis(i1, si // PEER_TOPK, axis=-1) * PEER_KEYS
         + jnp.take_along_axis(i2, si % PEER_TOPK, axis=-1))
    g = jax.nn.softmax(sv, axis=-1)
    e = e.reshape(T, PEER_HEADS * PEER_TOPK)
    g = g.reshape(T, PEER_HEADS * PEER_TOPK).astype(h.dtype)
    nb = -(-T // PEER_BLOCK)
    pad = nb * PEER_BLOCK - T
    hb = jnp.pad(hf, ((0, pad), (0, 0))).reshape(nb, PEER_BLOCK, D)
    eb = jnp.pad(e, ((0, pad), (0, 0))).reshape(nb, PEER_BLOCK, PEER_HEADS * PEER_TOPK)
    gb = jnp.pad(g, ((0, pad), (0, 0))).reshape(nb, PEER_BLOCK, PEER_HEADS * PEER_TOPK)
    def blk(args):
        hx, ex, gx = args
        act = jax.nn.gelu(jnp.einsum('td,tkd->tk', hx, u[ex])) * gx
        return jnp.einsum('tk,tkd->td', act, v[ex])
    out = lax.map(blk, (hb, eb, gb)).reshape(nb * PEER_BLOCK, D)[:T]
    return out.reshape(bsz, L, D)


def layer(x, c, ssd_h0, ssd_buf, lru_h0, lru_buf, ln1_w, ln2_w, w_mod, b_mod, w_in, ssd_conv_w, ssd_conv_b,
          ssd_dt_bias, ssd_a_log, ssd_d, ssd_norm_w, lru_conv_w, lru_conv_b, lru_wa, lru_ba, lru_wx, lru_bx,
          lru_lambda, w_br_ssd, w_br_lru, w_out, peer_wq, peer_k1, peer_k2, peer_u, peer_v):
    mod = (jax.nn.silu(c) @ w_mod + b_mod)[:, None, :]
    sh1, sc1, g1, sh2, sc2, g2 = jnp.split(mod, 6, axis=-1)
    h = rmsnorm(x, ln1_w) * (1.0 + sc1) + sh1
    tm, ssd_h, ssd_b, lru_h, lru_b = token_mixer(
        h, ssd_h0, ssd_buf, lru_h0, lru_buf, w_in, ssd_conv_w, ssd_conv_b, ssd_dt_bias, ssd_a_log, ssd_d,
        ssd_norm_w, lru_conv_w, lru_conv_b, lru_wa, lru_ba, lru_wx, lru_bx, lru_lambda, w_br_ssd, w_br_lru, w_out)
    x = x + g1 * tm
    h = rmsnorm(x, ln2_w) * (1.0 + sc2) + sh2
    x = x + g2 * peer(h, peer_wq, peer_k1, peer_k2, peer_u, peer_v)
    return x, ssd_h, ssd_b, lru_h, lru_b


def trunk(x, c, ssd_h, ssd_buf, lru_h, lru_buf, layer_w, final_norm_w):
    n_ssd, n_sbuf, n_lru, n_lbuf = [], [], [], []
    for l in range(DEPTH):
        wl = [w[l] for w in layer_w]
        x, sh, sb, lh, lb = layer(x, c, ssd_h[l], ssd_buf[l], lru_h[l], lru_buf[l], *wl)
        n_ssd.append(sh)
        n_sbuf.append(sb)
        n_lru.append(lh)
        n_lbuf.append(lb)
    y = rmsnorm(x, final_norm_w)
    return y, jnp.stack(n_ssd), jnp.stack(n_sbuf), jnp.stack(n_lru), jnp.stack(n_lbuf)


def setup_inputs(seed: int = 0) -> dict:
    key = jax.random.key(seed)
    ks = jax.random.split(key, 40)
    f32 = jnp.float32
    D = D_MODEL
    nrm = lambda k, shape, s: jax.random.normal(k, shape, f32) * s
    dt0 = jnp.exp(jax.random.uniform(ks[15], (DEPTH, SSD_HEADS), f32, np.log(1e-3), np.log(1e-1)))
    a0 = jax.random.uniform(ks[22], (DEPTH, LRU_WIDTH), f32, 0.9, 0.999)
    s0 = a0 ** (1.0 / LRU_C)
    return {
        'x_prompt': nrm(ks[0], (BATCH, SEQ, D), 1.0),
        'x_sample': nrm(ks[1], (DEC_BATCH, DEC_SEQ, D), 1.0),
        'c_prompt': nrm(ks[2], (BATCH, D), 1.0),
        'c_sample': nrm(ks[3], (DEC_BATCH, D), 1.0),
        'state_ssd': nrm(ks[4], (DEPTH, DEC_BATCH, SSD_HEADS, SSD_HEAD_DIM, SSD_STATE), 0.3),
        'state_ssd_conv': nrm(ks[5], (DEPTH, DEC_BATCH, CONV_W - 1, SSD_CONV_DIM), 1.0),
        'state_lru': nrm(ks[6], (DEPTH, DEC_BATCH, LRU_WIDTH), 0.5),
        'state_lru_conv': nrm(ks[7], (DEPTH, DEC_BATCH, CONV_W - 1, LRU_WIDTH), 1.0),
        'ln1_w': 1.0 + nrm(ks[8], (DEPTH, D), 0.02),
        'ln2_w': 1.0 + nrm(ks[9], (DEPTH, D), 0.02),
        'w_mod': nrm(ks[10], (DEPTH, D, 6 * D), 0.5 * D ** -0.5),
        'b_mod': nrm(ks[11], (DEPTH, 6 * D), 0.02),
        'w_in': nrm(ks[12], (DEPTH, D, IN_COLS), D ** -0.5),
        'ssd_conv_w': nrm(ks[13], (DEPTH, CONV_W, SSD_CONV_DIM), 0.5),
        'ssd_conv_b': nrm(ks[14], (DEPTH, SSD_CONV_DIM), 0.02),
        'ssd_dt_bias': dt0 + jnp.log(-jnp.expm1(-dt0)),
        'ssd_a_log': jnp.log(jax.random.uniform(ks[16], (DEPTH, SSD_HEADS), f32, 1.0, 16.0)),
        'ssd_d': 1.0 + nrm(ks[17], (DEPTH, SSD_HEADS), 0.02),
        'ssd_norm_w': 1.0 + nrm(ks[18], (DEPTH, SSD_INNER), 0.02),
        'lru_conv_w': nrm(ks[19], (DEPTH, CONV_W, LRU_WIDTH), 0.5),
        'lru_conv_b': nrm(ks[20], (DEPTH, LRU_WIDTH), 0.02),
        'lru_wa': nrm(ks[21], (DEPTH, LRU_BLOCKS, LRU_BLOCK_DIM, LRU_BLOCK_DIM), LRU_BLOCK_DIM ** -0.5),
        'lru_ba': nrm(ks[23], (DEPTH, LRU_WIDTH), 0.02),
        'lru_wx': nrm(ks[24], (DEPTH, LRU_BLOCKS, LRU_BLOCK_DIM, LRU_BLOCK_DIM), LRU_BLOCK_DIM ** -0.5),
        'lru_bx': nrm(ks[25], (DEPTH, LRU_WIDTH), 0.02),
        'lru_lambda': jnp.log(s0) - jnp.log1p(-s0),
        'w_br_ssd': nrm(ks[26], (DEPTH, SSD_INNER, D), SSD_INNER ** -0.5),
        'w_br_lru': nrm(ks[27], (DEPTH, LRU_WIDTH, D), LRU_WIDTH ** -0.5),
        'w_out': nrm(ks[28], (DEPTH, D, D), D ** -0.5),
        'peer_wq': nrm(ks[29], (DEPTH, D, PEER_HEADS * PEER_KEY_DIM), D ** -0.5),
        'peer_k1': nrm(ks[30], (DEPTH, PEER_KEYS, PEER_KEY_DIM // 2), (PEER_KEY_DIM // 2) ** -0.5),
        'peer_k2': nrm(ks[31], (DEPTH, PEER_KEYS, PEER_KEY_DIM // 2), (PEER_KEY_DIM // 2) ** -0.5),
        'peer_u': nrm(ks[32], (DEPTH, PEER_EXPERTS, D), D ** -0.5),
        'peer_v': nrm(ks[33], (DEPTH, PEER_EXPERTS, D), 0.2),
        'final_norm_w': 1.0 + nrm(ks[34], (D,), 0.02),
    }


def reference(x_prompt, x_sample, c_prompt, c_sample, state_ssd, state_ssd_conv, state_lru, state_lru_conv,
              ln1_w, ln2_w, w_mod, b_mod, w_in, ssd_conv_w, ssd_conv_b, ssd_dt_bias, ssd_a_log, ssd_d,
              ssd_norm_w, lru_conv_w, lru_conv_b, lru_wa, lru_ba, lru_wx, lru_bx, lru_lambda, w_br_ssd,
              w_br_lru, w_out, peer_wq, peer_k1, peer_k2, peer_u, peer_v, final_norm_w):
    layer_w = (ln1_w, ln2_w, w_mod, b_mod, w_in, ssd_conv_w, ssd_conv_b, ssd_dt_bias, ssd_a_log, ssd_d,
               ssd_norm_w, lru_conv_w, lru_conv_b, lru_wa, lru_ba, lru_wx, lru_bx, lru_lambda, w_br_ssd,
               w_br_lru, w_out, peer_wq, peer_k1, peer_k2, peer_u, peer_v)
    bp = x_prompt.shape[0]
    z_ssd = jnp.zeros((DEPTH, bp, SSD_HEADS, SSD_HEAD_DIM, SSD_STATE), jnp.float32)
    z_sconv = jnp.zeros((DEPTH, bp, CONV_W - 1, SSD_CONV_DIM), x_prompt.dtype)
    z_lru = jnp.zeros((DEPTH, bp, LRU_WIDTH), jnp.float32)
    z_lconv = jnp.zeros((DEPTH, bp, CONV_W - 1, LRU_WIDTH), x_prompt.dtype)
    y_prompt, p_ssd, p_sconv, p_lru, p_lconv = trunk(
        x_prompt, c_prompt, z_ssd, z_sconv, z_lru, z_lconv, layer_w, final_norm_w)
    y_sample, s_ssd, s_sconv, s_lru, s_lconv = trunk(
        x_sample, c_sample, state_ssd, state_ssd_conv, state_lru, state_lru_conv, layer_w, final_norm_w)
    return (y_prompt, y_sample, p_ssd, p_sconv, p_lru, p_lconv, s_ssd, s_sconv, s_lru, s_lconv)
```

```python
import functools

import jax
import jax.numpy as jnp
from jax import lax
from jax.experimental import pallas as pl
from jax.experimental.pallas import tpu as pltpu

f32 = jnp.float32
bf16 = jnp.bfloat16
HI = lax.Precision.HIGHEST

D_MODEL = 1024
DEPTH = 2
CONV_W = 4
EPS = 1e-6
SSD_HEADS = 16
SSD_HEAD_DIM = 64
SSD_INNER = SSD_HEADS * SSD_HEAD_DIM
SSD_GROUPS = 2
SSD_STATE = 64
SSD_BC = SSD_GROUPS * SSD_STATE
SSD_CONV_DIM = SSD_INNER + 2 * SSD_BC
LRU_WIDTH = 1024
LRU_BLOCKS = 16
LRU_C = 8.0
PEER_HEADS = 8
PEER_KEYS = 128
PEER_EXPERTS = PEER_KEYS * PEER_KEYS
PEER_KEY_DIM = 256
PEER_TOPK = 16
PEER_SEL = PEER_HEADS * PEER_TOPK
Z_END = SSD_INNER
XBC_END = Z_END + SSD_CONV_DIM
DT_END = XBC_END + SSD_HEADS
LX_END = DT_END + LRU_WIDTH
LY_END = LX_END + LRU_WIDTH
GS_END = LY_END + D_MODEL

LANES = 128
SUBLANES = 8
SSD_CHUNK = 128
ROW_TILE = 256
UV_ROWS = 2 * D_MODEL // LANES
VMEM_LIMIT = 56 * 1024 * 1024
NEG = -3.0e38
BIG = 1.0e9


def _sigmoid(x):
    return 1.0 / (1.0 + jnp.exp(-x))


def _silu(x):
    return x * _sigmoid(x)


def _gelu(x):
    return 0.5 * x * (1.0 + jnp.tanh(0.7978845608028654 * (x + 0.044715 * (x * x * x))))


def _softplus(x):
    return jnp.maximum(x, 0.0) + jnp.log(1.0 + jnp.exp(-jnp.abs(x)))


def _rms(x, w):
    return x * lax.rsqrt(jnp.mean(x * x, axis=-1, keepdims=True) + EPS) * w


def _params(sem):
    return pltpu.CompilerParams(dimension_semantics=sem, vmem_limit_bytes=VMEM_LIMIT)


def _row_tiling(B, L):
    if L >= ROW_TILE:
        assert L % ROW_TILE == 0
        return 1, ROW_TILE
    bt = min(B, ROW_TILE // L)
    assert B % bt == 0 and L % SUBLANES == 0
    return bt, L


def _mod_kernel(c_ref, w_ref, b_ref, o_ref):
    c = c_ref[...]
    o_ref[0] = jnp.dot(_silu(c), w_ref[0], preferred_element_type=f32, precision=HI) + b_ref[0]


def _mod(c, w_mod, b_mod):
    B = c.shape[0]
    n = w_mod.shape[-1]
    tn = 1536
    return pl.pallas_call(
        _mod_kernel,
        grid=(DEPTH, n // tn),
        in_specs=[pl.BlockSpec((B, D_MODEL), lambda l, j: (0, 0)),
                  pl.BlockSpec((1, D_MODEL, tn), lambda l, j: (l, 0, j)),
                  pl.BlockSpec((1, 1, tn), lambda l, j: (l, 0, j))],
        out_specs=pl.BlockSpec((1, B, tn), lambda l, j: (l, 0, j)),
        out_shape=jax.ShapeDtypeStruct((DEPTH, B, n), f32),
        compiler_params=_params(("arbitrary", "arbitrary")),
        name="mod",
    )(c, w_mod, b_mod.reshape(DEPTH, 1, n))


def _inproj_kernel(x_ref, sc_ref, sh_ref, lnw_ref, *refs):
    n = len(refs) // 2
    w_refs, o_refs = refs[:n], refs[n:]
    bt, lt, _ = x_ref.shape
    h = _rms(x_ref[...], lnw_ref[...]) * (1.0 + sc_ref[...]) + sh_ref[...]
    hb = h.reshape(bt * lt, D_MODEL).astype(bf16)
    for w_ref, o_ref in zip(w_refs, o_refs):
        o_ref[...] = jnp.dot(hb, w_ref[...], preferred_element_type=f32).reshape(o_ref.shape)


def _inproj(x, sc, sh, lnw, ws):
    B, L, _ = x.shape
    bt, lt = _row_tiling(B, L)
    row = lambda i, j: (i, j, 0)
    const = lambda i, j: (0, 0)
    return pl.pallas_call(
        _inproj_kernel,
        grid=(B // bt, L // lt),
        in_specs=[pl.BlockSpec((bt, lt, D_MODEL), row),
                  pl.BlockSpec((bt, 1, D_MODEL), lambda i, j: (i, 0, 0)),
                  pl.BlockSpec((bt, 1, D_MODEL), lambda i, j: (i, 0, 0)),
                  pl.BlockSpec((1, D_MODEL), const)]
                 + [pl.BlockSpec(w.shape, const) for w in ws],
        out_specs=[pl.BlockSpec((bt, lt, w.shape[1]), row) for w in ws],
        out_shape=[jax.ShapeDtypeStruct((B, L, w.shape[1]), f32) for w in ws],
        compiler_params=_params(("arbitrary", "arbitrary")),
        name="inproj",
    )(x, sc, sh, lnw, *ws)


def _conv_step(x_ref, buf0_ref, w_ref, b_ref, xpad, first):
    lv = x_ref.shape[1]

    @pl.when(first)
    def _():
        xpad[5:8, :] = buf0_ref[0]

    xpad[8:8 + lv, :] = x_ref[0]
    acc = b_ref[...] + w_ref[0:1, :] * xpad[5:5 + lv, :]
    for k in range(1, CONV_W):
        acc = acc + w_ref[k:k + 1, :] * xpad[5 + k:5 + k + lv, :]
    hist = xpad[5 + lv:8 + lv, :]
    xpad[5:8, :] = hist
    return acc, hist


def _pad_rows(x, rows):
    if x.shape[0] == rows:
        return x
    return jnp.concatenate([x, jnp.zeros((rows - x.shape[0], x.shape[1]), x.dtype)], axis=0)


def _ssd_kernel(xbc_ref, dt_ref, st0_ref, buf0_ref, cw_ref, cb_ref, dtb_ref, alog_ref, dvec_ref,
                expand_ref, gmask_ref, y_ref, st_ref, buf_ref, xpad, st):
    c = pl.program_id(1)
    last = c == pl.num_programs(1) - 1
    lv = xbc_ref.shape[1]
    Q = SSD_CHUNK

    @pl.when(c == 0)
    def _():
        s0 = st0_ref[0]
        st[...] = jnp.concatenate([s0, s0], axis=0) * gmask_ref[...]

    acc, hist = _conv_step(xbc_ref, buf0_ref, cw_ref, cb_ref, xpad, c == 0)
    xbc = _pad_rows(_silu(acc), Q)
    xs = xbc[:, :SSD_INNER]
    bm = xbc[:, SSD_INNER:SSD_INNER + SSD_BC]
    cm = xbc[:, SSD_INNER + SSD_BC:]

    lane = lax.broadcasted_iota(jnp.int32, (Q, LANES), 1)
    lane_v = lax.broadcasted_iota(jnp.int32, (lv, LANES), 1)
    dtv = _pad_rows(jnp.where(lane_v < SSD_HEADS, _softplus(dt_ref[0] + dtb_ref[...]), 0.0), Q)
    da = dtv * (-jnp.exp(alog_ref[...]))
    row = lax.broadcasted_iota(jnp.int32, (Q, Q), 0)
    col = lax.broadcasted_iota(jnp.int32, (Q, Q), 1)
    causal = row >= col
    acum = jnp.dot(causal.astype(f32), da, preferred_element_type=f32, precision=HI)
    acum_t = acum.T
    dt_t = dtv.T
    bm_t = bm.T
    alast = acum[Q - 1:Q, :]
    expand = expand_ref[...]
    ex = lambda v: jnp.dot(v, expand, preferred_element_type=f32, precision=HI)
    w_end = ex(jnp.exp(alast - acum) * dtv)
    e_in = ex(jnp.exp(acum))
    small = jnp.concatenate([jnp.exp(alast), dvec_ref[...], jnp.zeros((SUBLANES - 2, LANES), f32)], axis=0)
    small = ex(small)
    c_dec, d_full = small[0:1, :], small[1:2, :]

    st_in = st[...]
    y_off = jnp.dot(cm.astype(bf16), st_in.astype(bf16), preferred_element_type=f32) * e_in
    st_new = jnp.dot(bm_t.astype(bf16), (xs * w_end).astype(bf16), preferred_element_type=f32)
    st[...] = c_dec * st_in + st_new * gmask_ref[...]

    cb16, bm16 = cm.astype(bf16), bm.astype(bf16)
    nt = (((1,), (1,)), ((), ()))
    cbs = [lax.dot_general(jnp.where((lane >= g * SSD_STATE) & (lane < (g + 1) * SSD_STATE), cb16, 0), bm16, nt,
                           preferred_element_type=f32) for g in range(SSD_GROUPS)]
    hpg = SSD_HEADS // SSD_GROUPS
    ys = []
    for j in range(SSD_HEADS // 2):
        xp = xs[:, j * LANES:(j + 1) * LANES]
        wts, xsel = [], []
        for s in range(2):
            h = 2 * j + s
            seg = acum[:, h:h + 1] - acum_t[h:h + 1, :]
            dec = jnp.exp(jnp.where(causal, seg, NEG))
            wts.append((cbs[h // hpg] * dec * dt_t[h:h + 1, :]).astype(bf16))
            half = (lane >= s * SSD_HEAD_DIM) & (lane < (s + 1) * SSD_HEAD_DIM)
            xsel.append(jnp.where(half, xp, 0.0).astype(bf16))
        ys.append(jnp.dot(jnp.concatenate(wts, axis=1), jnp.concatenate(xsel, axis=0),
                          preferred_element_type=f32))
    y = jnp.concatenate(ys, axis=1) + y_off + d_full * xs
    y_ref[0] = y[:lv]

    @pl.when(last)
    def _():
        st_ref[0] = st[0:SSD_STATE, :] + st[SSD_STATE:, :]
        buf_ref[0] = hist


def _ssd(xbc, dt, st0, buf0, cw, cb, dtb, alog, dvec, expand, gmask):
    B, L, _ = xbc.shape
    lv = min(L, SSD_CHUNK)
    assert L % lv == 0
    row = lambda b, c: (b, c, 0)
    bat = lambda b, c: (b, 0, 0)
    const = lambda b, c: (0, 0)
    return pl.pallas_call(
        _ssd_kernel,
        grid=(B, L // lv),
        in_specs=[pl.BlockSpec((1, lv, SSD_CONV_DIM), row),
                  pl.BlockSpec((1, lv, LANES), row),
                  pl.BlockSpec((1, SSD_STATE, SSD_INNER), bat),
                  pl.BlockSpec((1, CONV_W - 1, SSD_CONV_DIM), bat),
                  pl.BlockSpec((CONV_W, SSD_CONV_DIM), const),
                  pl.BlockSpec((1, SSD_CONV_DIM), const),
                  pl.BlockSpec((1, LANES), const),
                  pl.BlockSpec((1, LANES), const),
                  pl.BlockSpec((1, LANES), const),
                  pl.BlockSpec((LANES, SSD_INNER), const),
                  pl.BlockSpec((LANES, SSD_INNER), const)],
        out_specs=[pl.BlockSpec((1, lv, SSD_INNER), row),
                   pl.BlockSpec((1, SSD_STATE, SSD_INNER), bat),
                   pl.BlockSpec((1, CONV_W - 1, SSD_CONV_DIM), bat)],
        out_shape=[jax.ShapeDtypeStruct((B, L, SSD_INNER), f32),
                   jax.ShapeDtypeStruct((B, SSD_STATE, SSD_INNER), f32),
                   jax.ShapeDtypeStruct((B, CONV_W - 1, SSD_CONV_DIM), f32)],
        scratch_shapes=[pltpu.VMEM((SUBLANES + lv, SSD_CONV_DIM), f32),
                        pltpu.VMEM((LANES, SSD_INNER), f32)],
        compiler_params=_params(("arbitrary", "arbitrary")),
        name="ssd",
    )(xbc, dt, st0, buf0, cw, cb, dtb, alog, dvec, expand, gmask)


def _lru_kernel(lx_ref, h0_ref, buf0_ref, cw_ref, cb_ref, wa_ref, ba_ref, wx_ref, bx_ref, lam_ref,
                hs_ref, hout_ref, buf_ref, xpad, a_s, u_s, hcar):
    c = pl.program_id(1)
    last = c == pl.num_programs(1) - 1
    lv = lx_ref.shape[1]

    @pl.when(c == 0)
    def _():
        hcar[0:1, :] = h0_ref[0]

    xb, hist = _conv_step(lx_ref, buf0_ref, cw_ref, cb_ref, xpad, c == 0)
    x16 = xb.astype(bf16)
    r = _sigmoid(jnp.dot(x16, wa_ref[...], preferred_element_type=f32) + ba_ref[...])
    i = _sigmoid(jnp.dot(x16, wx_ref[...], preferred_element_type=f32) + bx_ref[...])
    log_a = -LRU_C * r * _softplus(-lam_ref[...])
    a_s[...] = jnp.exp(log_a)
    u_s[...] = jnp.sqrt(1.0 - jnp.exp(2.0 * log_a)) * (i * xb)

    def body(j, h):
        r0 = pl.multiple_of(j * SUBLANES, SUBLANES)
        a8 = a_s[pl.ds(r0, SUBLANES), :]
        u8 = u_s[pl.ds(r0, SUBLANES), :]
        rows = []
        for k in range(SUBLANES):
            h = a8[k:k + 1, :] * h + u8[k:k + 1, :]
            rows.append(h)
        hs_ref[0, pl.ds(r0, SUBLANES), :] = jnp.concatenate(rows, axis=0)
        return h

    h = lax.fori_loop(0, lv // SUBLANES, body, hcar[0:1, :])
    hcar[0:1, :] = h

    @pl.when(last)
    def _():
        hout_ref[0] = h
        buf_ref[0] = hist


def _lru(lx, h0, buf0, cw, cb, wa, ba, wx, bx, lam):
    B, L, _ = lx.shape
    lv = min(L, SSD_CHUNK)
    assert L % lv == 0 and lv % SUBLANES == 0
    row = lambda b, c: (b, c, 0)
    bat = lambda b, c: (b, 0, 0)
    const = lambda b, c: (0, 0)
    W = LRU_WIDTH
    return pl.pallas_call(
        _lru_kernel,
        grid=(B, L // lv),
        in_specs=[pl.BlockSpec((1, lv, W), row),
                  pl.BlockSpec((1, 1, W), bat),
                  pl.BlockSpec((1, CONV_W - 1, W), bat),
                  pl.BlockSpec((CONV_W, W), const),
                  pl.BlockSpec((1, W), const),
                  pl.BlockSpec((W, W), const),
                  pl.BlockSpec((1, W), const),
                  pl.BlockSpec((W, W), const),
                  pl.BlockSpec((1, W), const),
                  pl.BlockSpec((1, W), const)],
        out_specs=[pl.BlockSpec((1, lv, W), row),
                   pl.BlockSpec((1, 1, W), bat),
                   pl.BlockSpec((1, CONV_W - 1, W), bat)],
        out_shape=[jax.ShapeDtypeStruct((B, L, W), f32),
                   jax.ShapeDtypeStruct((B, 1, W), f32),
                   jax.ShapeDtypeStruct((B, CONV_W - 1, W), f32)],
        scratch_shapes=[pltpu.VMEM((SUBLANES + lv, W), f32),
                        pltpu.VMEM((lv, W), f32),
                        pltpu.VMEM((lv, W), f32),
                        pltpu.VMEM((SUBLANES, W), f32)],
        compiler_params=_params(("arbitrary", "arbitrary")),
        name="lru",
    )(lx, h0, buf0, cw, cb, wa, ba, wx, bx, lam)


def _merge_kernel(y_ref, z_ref, hs_ref, ly_ref, gs_ref, gl_ref, x_ref, g1_ref, sc2_ref, sh2_ref,
                  nw_ref, ln2_ref, wbs_ref, wbl_ref, wo_ref, wq_ref, x1_ref, h2_ref, q_ref):
    bt, lt, _ = x_ref.shape
    rows = bt * lt
    flat = lambda v: v.reshape(rows, v.shape[-1])
    ys = _rms(y_ref[...] * _silu(z_ref[...]), nw_ref[...])
    yl = hs_ref[...] * _gelu(ly_ref[...])
    ps = jnp.dot(flat(ys).astype(bf16), wbs_ref[...], preferred_element_type=f32)
    plr = jnp.dot(flat(yl).astype(bf16), wbl_ref[...], preferred_element_type=f32)
    merged = _sigmoid(flat(gs_ref[...])) * ps + _sigmoid(flat(gl_ref[...])) * plr
    tm = jnp.dot(merged.astype(bf16), wo_ref[...], preferred_element_type=f32)
    x1 = x_ref[...] + g1_ref[...] * tm.reshape(bt, lt, D_MODEL)
    x1_ref[...] = x1
    h2 = _rms(x1, ln2_ref[...]) * (1.0 + sc2_ref[...]) + sh2_ref[...]
    h2_ref[...] = h2
    q = jnp.dot(flat(h2).astype(bf16), wq_ref[...], preferred_element_type=f32)
    for h in range(PEER_HEADS):
        q_ref[h] = q[:, h * PEER_KEY_DIM:(h + 1) * PEER_KEY_DIM].reshape(bt, lt, PEER_KEY_DIM)


def _merge(y, z, hs, ly, gs, gl, x, g1, sc2, sh2, nw, ln2, wbs, wbl, wo, wq):
    B, L, _ = x.shape
    bt, lt = _row_tiling(B, L)
    row = lambda i, j: (i, j, 0)
    bat = lambda i, j: (i, 0, 0)
    const = lambda i, j: (0, 0)
    act = pl.BlockSpec((bt, lt, D_MODEL), row)
    vec = pl.BlockSpec((bt, 1, D_MODEL), bat)
    return pl.pallas_call(
        _merge_kernel,
        grid=(B // bt, L // lt),
        in_specs=[act] * 7 + [vec] * 3
                 + [pl.BlockSpec((1, D_MODEL), const)] * 2
                 + [pl.BlockSpec((D_MODEL, D_MODEL), const)] * 3
                 + [pl.BlockSpec((D_MODEL, PEER_HEADS * PEER_KEY_DIM), const)],
        out_specs=[act, act,
                   pl.BlockSpec((PEER_HEADS, bt, lt, PEER_KEY_DIM), lambda i, j: (0, i, j, 0))],
        out_shape=[jax.ShapeDtypeStruct((B, L, D_MODEL), f32),
                   jax.ShapeDtypeStruct((B, L, D_MODEL), f32),
                   jax.ShapeDtypeStruct((PEER_HEADS, B, L, PEER_KEY_DIM), f32)],
        compiler_params=_params(("arbitrary", "arbitrary")),
        name="merge",
    )(y, z, hs, ly, gs, gl, x, g1, sc2, sh2, nw, ln2, wbs, wbl, wo, wq)


def _top16(s, idx):
    vals, ids = [], []
    for _ in range(PEER_TOPK):
        m = jnp.max(s, axis=0, keepdims=True)
        sel = jnp.min(jnp.where(s == m, idx, BIG), axis=0, keepdims=True)
        s = jnp.where(idx == sel, NEG, s)
        vals.append(m)
        ids.append(sel)
    return jnp.concatenate(vals, axis=0), jnp.concatenate(ids, axis=0)


def _topk_kernel(q_ref, k1_ref, k2_ref, e_ref, g_ref):
    tt = q_ref.shape[1]
    half = PEER_KEY_DIM // 2
    q = q_ref[0]
    nt = (((1,), (1,)), ((), ()))
    s1 = lax.dot_general(k1_ref[...], q[:, :half], nt, preferred_element_type=f32, precision=HI)
    s2 = lax.dot_general(k2_ref[...], q[:, half:], nt, preferred_element_type=f32, precision=HI)
    key = lax.broadcasted_iota(jnp.int32, (PEER_KEYS, tt), 0).astype(f32)
    v1, i1 = _top16(s1, key)
    v2, i2 = _top16(s2, key)
    cand = jnp.concatenate([v1[a:a + 1, :] + v2 for a in range(PEER_TOPK)], axis=0)
    eidx = jnp.concatenate([i1[a:a + 1, :] * float(PEER_KEYS) + i2 for a in range(PEER_TOPK)], axis=0)
    flat = lax.broadcasted_iota(jnp.int32, (PEER_TOPK * PEER_TOPK, tt), 0).astype(f32)
    svs, es = [], []
    s = cand
    for _ in range(PEER_TOPK):
        m = jnp.max(s, axis=0, keepdims=True)
        sel = jnp.min(jnp.where(s == m, flat, BIG), axis=0, keepdims=True)
        hit = flat == sel
        es.append(jnp.max(jnp.where(hit, eidx, -1.0), axis=0, keepdims=True))
        s = jnp.where(hit, NEG, s)
        svs.append(m)
    sv = jnp.concatenate(svs, axis=0)
    p = jnp.exp(sv - sv[0:1, :])
    g_ref[...] = p / jnp.sum(p, axis=0, keepdims=True)
    e_ref[...] = jnp.concatenate(es, axis=0).astype(jnp.int32)


def _topk(qh, k1, k2):
    H, T, _ = qh.shape
    tt = LANES
    assert T % tt == 0
    return pl.pallas_call(
        _topk_kernel,
        grid=(T // tt, H),
        in_specs=[pl.BlockSpec((1, tt, PEER_KEY_DIM), lambda i, h: (h, i, 0)),
                  pl.BlockSpec((PEER_KEYS, PEER_KEY_DIM // 2), lambda i, h: (0, 0)),
                  pl.BlockSpec((PEER_KEYS, PEER_KEY_DIM // 2), lambda i, h: (0, 0))],
        out_specs=[pl.BlockSpec((PEER_TOPK, tt), lambda i, h: (h, i)),
                   pl.BlockSpec((PEER_TOPK, tt), lambda i, h: (h, i))],
        out_shape=[jax.ShapeDtypeStruct((PEER_SEL, T), jnp.int32),
                   jax.ShapeDtypeStruct((PEER_SEL, T), f32)],
        compiler_params=_params(("arbitrary", "arbitrary")),
        name="topk",
    )(qh, k1, k2)


def _peer_kernel(h_ref, x1_ref, gt_ref, g2_ref, fw_ref, e_hbm, uv_hbm, o_ref,
                 buf0, buf1, esm, sem_uv, sem_e, *, rows_per_batch, final):
    tm = h_ref.shape[0]
    i = pl.program_id(0)
    ecp = pltpu.make_async_copy(e_hbm.at[pl.ds(i * (tm * PEER_SEL), tm * PEER_SEL)], esm, sem_e)
    ecp.start()
    ecp.wait()
    bufs = (buf0, buf1)

    def row_copy(t, k, slot):
        r0 = pl.multiple_of(esm[t * PEER_SEL + k] * UV_ROWS, UV_ROWS)
        return pltpu.make_async_copy(uv_hbm.at[pl.ds(r0, UV_ROWS), :],
                                     bufs[slot].at[pl.ds(k * UV_ROWS, UV_ROWS), :], sem_uv.at[slot])

    def issue(t, slot):
        for k in range(PEER_SEL):
            row_copy(t, k, slot).start()

    def wait(slot):
        pltpu.make_async_copy(uv_hbm.at[pl.ds(0, PEER_SEL * UV_ROWS), :], bufs[slot], sem_uv.at[slot]).wait()

    lane = lax.broadcasted_iota(jnp.int32, (PEER_SEL, tm), 1)
    nch = D_MODEL // LANES

    def compute(t, slot):
        buf = bufs[slot]
        hrow = h_ref[pl.ds(t, 1), :]
        acc = buf[pl.ds(0, PEER_SEL, stride=UV_ROWS), :] * hrow[:, 0:LANES]
        for c in range(1, nch):
            acc = acc + buf[pl.ds(c, PEER_SEL, stride=UV_ROWS), :] * hrow[:, c * LANES:(c + 1) * LANES]
        s = jnp.sum(acc, axis=1, keepdims=True)
        gcol = jnp.sum(jnp.where(lane == t, gt_ref[...], 0.0), axis=1, keepdims=True)
        a = _gelu(s) * gcol
        outs = [jnp.sum(buf[pl.ds(nch + c, PEER_SEL, stride=UV_ROWS), :] * a, axis=0, keepdims=True)
                for c in range(nch)]
        o = jnp.concatenate(outs, axis=1)
        g2 = g2_ref[t // rows_per_batch] if rows_per_batch < tm else g2_ref[0]
        x2 = x1_ref[pl.ds(t, 1), :] + g2 * o
        if final:
            x2 = _rms(x2, fw_ref[...])
        o_ref[pl.ds(t, 1), :] = x2

    issue(0, 0)

    def pair(j, carry):
        t0 = 2 * j
        issue(t0 + 1, 1)
        wait(0)
        compute(t0, 0)

        @pl.when(t0 + 2 < tm)
        def _():
            issue(t0 + 2, 0)

        wait(1)
        compute(t0 + 1, 1)
        return carry

    lax.fori_loop(0, tm // 2, pair, 0)


def _peer(h2, x1, gt, g2, fw, e_flat, uv, rows_per_batch, final):
    T = h2.shape[0]
    tm = LANES
    assert T % tm == 0
    if rows_per_batch >= tm:
        assert rows_per_batch % tm == 0
        g2_spec = pl.BlockSpec((1, 1, D_MODEL), lambda i: (i * tm // rows_per_batch, 0, 0))
    else:
        assert tm % rows_per_batch == 0
        nb = tm // rows_per_batch
        g2_spec = pl.BlockSpec((nb, 1, D_MODEL), lambda i: (i, 0, 0))
    tok = pl.BlockSpec((tm, D_MODEL), lambda i: (i, 0))
    return pl.pallas_call(
        functools.partial(_peer_kernel, rows_per_batch=rows_per_batch, final=final),
        grid=(T // tm,),
        in_specs=[tok, tok,
                  pl.BlockSpec((PEER_SEL, tm), lambda i: (0, i)),
                  g2_spec,
                  pl.BlockSpec((1, D_MODEL), lambda i: (0, 0)),
                  pl.BlockSpec(memory_space=pl.ANY),
                  pl.BlockSpec(memory_space=pl.ANY)],
        out_specs=tok,
        out_shape=jax.ShapeDtypeStruct((T, D_MODEL), f32),
        scratch_shapes=[pltpu.VMEM((PEER_SEL * UV_ROWS, LANES), f32),
                        pltpu.VMEM((PEER_SEL * UV_ROWS, LANES), f32),
                        pltpu.SMEM((tm * PEER_SEL,), jnp.int32),
                        pltpu.SemaphoreType.DMA((2,)),
                        pltpu.SemaphoreType.DMA(())],
        compiler_params=_params(("arbitrary",)),
        name="peer",
    )(h2, x1, gt, g2, fw, e_flat, uv)


def _block_diag(w):
    n, d, _ = w.shape
    eye = jnp.eye(n, dtype=w.dtype)
    return (eye[:, None, :, None] * w[:, :, None, :]).reshape(n * d, n * d)


def _pad_lanes(v):
    return jnp.pad(v, (0, LANES - v.shape[0])).reshape(1, LANES)


def _prep_layer(l, p):
    w_in = p["w_in"][l]
    ws = [w_in[:, :Z_END], w_in[:, Z_END:XBC_END],
          jnp.pad(w_in[:, XBC_END:DT_END], ((0, 0), (0, LANES - SSD_HEADS))),
          w_in[:, DT_END:LX_END], w_in[:, LX_END:LY_END], w_in[:, LY_END:GS_END], w_in[:, GS_END:]]
    head = jnp.arange(SSD_INNER) // SSD_HEAD_DIM
    expand = (jnp.arange(LANES)[:, None] == head[None, :]).astype(f32)
    gmask = ((jnp.arange(LANES)[:, None] // SSD_STATE) == (head[None, :] // (SSD_HEADS // SSD_GROUPS))).astype(f32)
    uv = jnp.concatenate([p["peer_u"][l].reshape(PEER_EXPERTS, UV_ROWS // 2, LANES),
                          p["peer_v"][l].reshape(PEER_EXPERTS, UV_ROWS // 2, LANES)], axis=1)
    return dict(
        ws=[w.astype(bf16) for w in ws],
        ln1=p["ln1_w"][l].reshape(1, -1), ln2=p["ln2_w"][l].reshape(1, -1),
        ssd_cw=p["ssd_conv_w"][l], ssd_cb=p["ssd_conv_b"][l].reshape(1, -1),
        dtb=_pad_lanes(p["ssd_dt_bias"][l]), alog=_pad_lanes(p["ssd_a_log"][l]), dvec=_pad_lanes(p["ssd_d"][l]),
        expand=expand, gmask=gmask, nw=p["ssd_norm_w"][l].reshape(1, -1),
        lru_cw=p["lru_conv_w"][l], lru_cb=p["lru_conv_b"][l].reshape(1, -1),
        wa=_block_diag(p["lru_wa"][l]).astype(bf16), ba=p["lru_ba"][l].reshape(1, -1),
        wx=_block_diag(p["lru_wx"][l]).astype(bf16), bx=p["lru_bx"][l].reshape(1, -1),
        lam=p["lru_lambda"][l].reshape(1, -1),
        wbs=p["w_br_ssd"][l].astype(bf16), wbl=p["w_br_lru"][l].astype(bf16), wo=p["w_out"][l].astype(bf16),
        wq=p["peer_wq"][l].astype(bf16), k1=p["peer_k1"][l], k2=p["peer_k2"][l],
        uv=uv.reshape(PEER_EXPERTS * UV_ROWS, LANES),
    )


def _state_in(s):
    return s.transpose(0, 3, 1, 2).reshape(s.shape[0], SSD_STATE, SSD_INNER)


def _state_out(s):
    return s.reshape(s.shape[0], SSD_STATE, SSD_HEADS, SSD_HEAD_DIM).transpose(0, 2, 3, 1)


def _layer(x, mod, ssd_h0, ssd_buf, lru_h0, lru_buf, w, fw, final):
    B, L, _ = x.shape
    sh1, sc1, g1, sh2, sc2, g2 = [m.reshape(B, 1, D_MODEL) for m in jnp.split(mod, 6, axis=-1)]
    z, xbc, dt, lx, ly, gs, gl = _inproj(x, sc1, sh1, w["ln1"], w["ws"])
    y, st, sbuf = _ssd(xbc, dt, _state_in(ssd_h0), ssd_buf, w["ssd_cw"], w["ssd_cb"], w["dtb"], w["alog"],
                       w["dvec"], w["expand"], w["gmask"])
    hs, lh, lbuf = _lru(lx, lru_h0.reshape(B, 1, LRU_WIDTH), lru_buf, w["lru_cw"], w["lru_cb"], w["wa"], w["ba"],
                        w["wx"], w["bx"], w["lam"])
    x1, h2, qh = _merge(y, z, hs, ly, gs, gl, x, g1, sc2, sh2, w["nw"], w["ln2"], w["wbs"], w["wbl"], w["wo"],
                        w["wq"])
    T = B * L
    e_t, g_t = _topk(qh.reshape(PEER_HEADS, T, PEER_KEY_DIM), w["k1"], w["k2"])
    x2 = _peer(h2.reshape(T, D_MODEL), x1.reshape(T, D_MODEL), g_t, g2, fw, e_t.T.reshape(T * PEER_SEL), w["uv"],
               L, final)
    return x2.reshape(B, L, D_MODEL), _state_out(st), sbuf, lh.reshape(B, LRU_WIDTH), lbuf


def _trunk(x, c, ssd_h, ssd_buf, lru_h, lru_buf, p, layers):
    mod = _mod(c, p["w_mod"], p["b_mod"])
    fw = p["final_norm_w"].reshape(1, -1)
    outs = [[], [], [], []]
    for l in range(DEPTH):
        x, *new = _layer(x, mod[l], ssd_h[l], ssd_buf[l], lru_h[l], lru_buf[l], layers[l], fw, l == DEPTH - 1)
        for acc, v in zip(outs, new):
            acc.append(v)
    return (x,) + tuple(jnp.stack(o) for o in outs)


def kernel(x_prompt, x_sample, c_prompt, c_sample, state_ssd, state_ssd_conv, state_lru, state_lru_conv, ln1_w, ln2_w, w_mod, b_mod, w_in, ssd_conv_w, ssd_conv_b, ssd_dt_bias, ssd_a_log, ssd_d, ssd_norm_w, lru_conv_w, lru_conv_b, lru_wa, lru_ba, lru_wx, lru_bx, lru_lambda, w_br_ssd, w_br_lru, w_out, peer_wq, peer_k1, peer_k2, peer_u, peer_v, final_norm_w):
    p = dict(ln1_w=ln1_w, ln2_w=ln2_w, w_mod=w_mod, b_mod=b_mod, w_in=w_in, ssd_conv_w=ssd_conv_w,
             ssd_conv_b=ssd_conv_b, ssd_dt_bias=ssd_dt_bias, ssd_a_log=ssd_a_log, ssd_d=ssd_d,
             ssd_norm_w=ssd_norm_w, lru_conv_w=lru_conv_w, lru_conv_b=lru_conv_b, lru_wa=lru_wa, lru_ba=lru_ba,
             lru_wx=lru_wx, lru_bx=lru_bx, lru_lambda=lru_lambda, w_br_ssd=w_br_ssd, w_br_lru=w_br_lru,
             w_out=w_out, peer_wq=peer_wq, peer_k1=peer_k1, peer_k2=peer_k2, peer_u=peer_u, peer_v=peer_v,
             final_norm_w=final_norm_w)
    layers = [_prep_layer(l, p) for l in range(DEPTH)]
    bp = x_prompt.shape[0]
    zeros = lambda *s: jnp.zeros((DEPTH, bp) + s, f32)
    ys = _trunk(x_sample, c_sample, state_ssd, state_ssd_conv, state_lru, state_lru_conv, p, layers)
    yp = _trunk(x_prompt, c_prompt, zeros(SSD_HEADS, SSD_HEAD_DIM, SSD_STATE), zeros(CONV_W - 1, SSD_CONV_DIM),
                zeros(LRU_WIDTH), zeros(CONV_W - 1, LRU_WIDTH), p, layers)
    return (yp[0], ys[0]) + yp[1:] + ys[1:]
```

```python
import functools

import jax
import jax.numpy as jnp
from jax import lax
from jax.experimental import pallas as pl
from jax.experimental.pallas import tpu as pltpu

f32 = jnp.float32
bf16 = jnp.bfloat16
HI = lax.Precision.HIGHEST

D_MODEL = 1024
DEPTH = 2
CONV_W = 4
EPS = 1e-6
SSD_HEADS = 16
SSD_HEAD_DIM = 64
SSD_INNER = SSD_HEADS * SSD_HEAD_DIM
SSD_GROUPS = 2
SSD_STATE = 64
SSD_BC = SSD_GROUPS * SSD_STATE
SSD_CONV_DIM = SSD_INNER + 2 * SSD_BC
LRU_WIDTH = 1024
LRU_BLOCKS = 16
LRU_C = 8.0
PEER_HEADS = 8
PEER_KEYS = 128
PEER_EXPERTS = PEER_KEYS * PEER_KEYS
PEER_KEY_DIM = 256
PEER_TOPK = 16
PEER_SEL = PEER_HEADS * PEER_TOPK
Z_END = SSD_INNER
XBC_END = Z_END + SSD_CONV_DIM
DT_END = XBC_END + SSD_HEADS
LX_END = DT_END + LRU_WIDTH
LY_END = LX_END + LRU_WIDTH
GS_END = LY_END + D_MODEL

LANES = 128
SUBLANES = 8
SSD_CHUNK = 128
ROW_TILE = 256
UV_ROWS = D_MODEL // LANES
VMEM_LIMIT = 56 * 1024 * 1024
NEG = -3.0e38
BIG = 1.0e9


def _sigmoid(x):
    return 1.0 / (1.0 + jnp.exp(-x))


def _silu(x):
    return x * _sigmoid(x)


def _gelu(x):
    return 0.5 * x * (1.0 + jnp.tanh(0.7978845608028654 * (x + 0.044715 * (x * x * x))))


def _softplus(x):
    return jnp.maximum(x, 0.0) + jnp.log(1.0 + jnp.exp(-jnp.abs(x)))


def _rms(x, w):
    return x * lax.rsqrt(jnp.mean(x * x, axis=-1, keepdims=True) + EPS) * w


def _params(sem):
    return pltpu.CompilerParams(dimension_semantics=sem, vmem_limit_bytes=VMEM_LIMIT)


def _row_tiling(B, L):
    if L >= ROW_TILE:
        assert L % ROW_TILE == 0
        return 1, ROW_TILE
    bt = min(B, ROW_TILE // L)
    assert B % bt == 0 and L % SUBLANES == 0
    return bt, L


def _mod_kernel(c_ref, w_ref, b_ref, o_ref):
    c = c_ref[...]
    o_ref[0] = jnp.dot(_silu(c), w_ref[0], preferred_element_type=f32, precision=HI) + b_ref[0]


def _mod(c, w_mod, b_mod):
    B = c.shape[0]
    n = w_mod.shape[-1]
    tn = 1536
    return pl.pallas_call(
        _mod_kernel,
        grid=(DEPTH, n // tn),
        in_specs=[pl.BlockSpec((B, D_MODEL), lambda l, j: (0, 0)),
                  pl.BlockSpec((1, D_MODEL, tn), lambda l, j: (l, 0, j)),
                  pl.BlockSpec((1, 1, tn), lambda l, j: (l, 0, j))],
        out_specs=pl.BlockSpec((1, B, tn), lambda l, j: (l, 0, j)),
        out_shape=jax.ShapeDtypeStruct((DEPTH, B, n), f32),
        compiler_params=_params(("arbitrary", "arbitrary")),
        name="mod",
    )(c, w_mod, b_mod.reshape(DEPTH, 1, n))


def _inproj_kernel(x_ref, sc_ref, sh_ref, lnw_ref, *refs):
    n = len(refs) // 2
    w_refs, o_refs = refs[:n], refs[n:]
    bt, lt, _ = x_ref.shape
    h = _rms(x_ref[...], lnw_ref[...]) * (1.0 + sc_ref[...]) + sh_ref[...]
    hb = h.reshape(bt * lt, D_MODEL).astype(bf16)
    for w_ref, o_ref in zip(w_refs, o_refs):
        o_ref[...] = jnp.dot(hb, w_ref[...], preferred_element_type=f32).reshape(o_ref.shape)


def _inproj(x, sc, sh, lnw, ws):
    B, L, _ = x.shape
    bt, lt = _row_tiling(B, L)
    row = lambda i, j: (i, j, 0)
    const = lambda i, j: (0, 0)
    return pl.pallas_call(
        _inproj_kernel,
        grid=(B // bt, L // lt),
        in_specs=[pl.BlockSpec((bt, lt, D_MODEL), row),
                  pl.BlockSpec((bt, 1, D_MODEL), lambda i, j: (i, 0, 0)),
                  pl.BlockSpec((bt, 1, D_MODEL), lambda i, j: (i, 0, 0)),
                  pl.BlockSpec((1, D_MODEL), const)]
                 + [pl.BlockSpec(w.shape, const) for w in ws],
        out_specs=[pl.BlockSpec((bt, lt, w.shape[1]), row) for w in ws],
        out_shape=[jax.ShapeDtypeStruct((B, L, w.shape[1]), f32) for w in ws],
        compiler_params=_params(("arbitrary", "arbitrary")),
        name="inproj",
    )(x, sc, sh, lnw, *ws)


def _conv_step(x_ref, buf0_ref, w_ref, b_ref, xpad, first):
    lv = x_ref.shape[1]

    @pl.when(first)
    def _():
        xpad[5:8, :] = buf0_ref[0]

    xpad[8:8 + lv, :] = x_ref[0]
    acc = b_ref[...] + w_ref[0:1, :] * xpad[5:5 + lv, :]
    for k in range(1, CONV_W):
        acc = acc + w_ref[k:k + 1, :] * xpad[5 + k:5 + k + lv, :]
    hist = xpad[5 + lv:8 + lv, :]
    xpad[5:8, :] = hist
    return acc, hist


def _pad_rows(x, rows):
    if x.shape[0] == rows:
        return x
    return jnp.concatenate([x, jnp.zeros((rows - x.shape[0], x.shape[1]), x.dtype)], axis=0)


def _ssd_kernel(xbc_ref, dt_ref, st0_ref, buf0_ref, cw_ref, cb_ref, dtb_ref, alog_ref, dvec_ref,
                expand_ref, gmask_ref, y_ref, st_ref, buf_ref, xpad, st):
    c = pl.program_id(1)
    last = c == pl.num_programs(1) - 1
    lv = xbc_ref.shape[1]
    Q = SSD_CHUNK

    @pl.when(c == 0)
    def _():
        s0 = st0_ref[0]
        st[...] = jnp.concatenate([s0, s0], axis=0) * gmask_ref[...]

    acc, hist = _conv_step(xbc_ref, buf0_ref, cw_ref, cb_ref, xpad, c == 0)
    xbc = _pad_rows(_silu(acc), Q)
    xs = xbc[:, :SSD_INNER]
    bm = xbc[:, SSD_INNER:SSD_INNER + SSD_BC]
    cm = xbc[:, SSD_INNER + SSD_BC:]

    lane = lax.broadcasted_iota(jnp.int32, (Q, LANES), 1)
    lane_v = lax.broadcasted_iota(jnp.int32, (lv, LANES), 1)
    dtv = _pad_rows(jnp.where(lane_v < SSD_HEADS, _softplus(dt_ref[0] + dtb_ref[...]), 0.0), Q)
    da = dtv * (-jnp.exp(alog_ref[...]))
    row = lax.broadcasted_iota(jnp.int32, (Q, Q), 0)
    col = lax.broadcasted_iota(jnp.int32, (Q, Q), 1)
    causal = row >= col
    acum = jnp.dot(causal.astype(f32), da, preferred_element_type=f32, precision=HI)
    acum_t = acum.T
    dt_t = dtv.T
    bm_t = bm.T
    alast = acum[Q - 1:Q, :]
    expand = expand_ref[...]
    ex = lambda v: jnp.dot(v, expand, preferred_element_type=f32, precision=HI)
    w_end = ex(jnp.exp(alast - acum) * dtv)
    e_in = ex(jnp.exp(acum))
    small = jnp.concatenate([jnp.exp(alast), dvec_ref[...], jnp.zeros((SUBLANES - 2, LANES), f32)], axis=0)
    small = ex(small)
    c_dec, d_full = small[0:1, :], small[1:2, :]

    st_in = st[...]
    y_off = jnp.dot(cm.astype(bf16), st_in.astype(bf16), preferred_element_type=f32) * e_in
    st_new = jnp.dot(bm_t.astype(bf16), (xs * w_end).astype(bf16), preferred_element_type=f32)
    st[...] = c_dec * st_in + st_new * gmask_ref[...]

    cb16, bm16 = cm.astype(bf16), bm.astype(bf16)
    nt = (((1,), (1,)), ((), ()))
    cbs = [lax.dot_general(jnp.where((lane >= g * SSD_STATE) & (lane < (g + 1) * SSD_STATE), cb16, 0), bm16, nt,
                           preferred_element_type=f32) for g in range(SSD_GROUPS)]
    hpg = SSD_HEADS // SSD_GROUPS
    ys = []
    for j in range(SSD_HEADS // 2):
        xp = xs[:, j * LANES:(j + 1) * LANES]
        wts, xsel = [], []
        for s in range(2):
            h = 2 * j + s
            seg = acum[:, h:h + 1] - acum_t[h:h + 1, :]
            dec = jnp.exp(jnp.where(causal, seg, NEG))
            wts.append((cbs[h // hpg] * dec * dt_t[h:h + 1, :]).astype(bf16))
            half = (lane >= s * SSD_HEAD_DIM) & (lane < (s + 1) * SSD_HEAD_DIM)
            xsel.append(jnp.where(half, xp, 0.0).astype(bf16))
        ys.append(jnp.dot(jnp.concatenate(wts, axis=1), jnp.concatenate(xsel, axis=0),
                          preferred_element_type=f32))
    y = jnp.concatenate(ys, axis=1) + y_off + d_full * xs
    y_ref[0] = y[:lv]

    @pl.when(last)
    def _():
        st_ref[0] = st[0:SSD_STATE, :] + st[SSD_STATE:, :]
        buf_ref[0] = hist


def _ssd(xbc, dt, st0, buf0, cw, cb, dtb, alog, dvec, expand, gmask):
    B, L, _ = xbc.shape
    lv = min(L, SSD_CHUNK)
    assert L % lv == 0
    row = lambda b, c: (b, c, 0)
    bat = lambda b, c: (b, 0, 0)
    const = lambda b, c: (0, 0)
    return pl.pallas_call(
        _ssd_kernel,
        grid=(B, L // lv),
        in_specs=[pl.BlockSpec((1, lv, SSD_CONV_DIM), row),
                  pl.BlockSpec((1, lv, LANES), row),
                  pl.BlockSpec((1, SSD_STATE, SSD_INNER), bat),
                  pl.BlockSpec((1, CONV_W - 1, SSD_CONV_DIM), bat),
                  pl.BlockSpec((CONV_W, SSD_CONV_DIM), const),
                  pl.BlockSpec((1, SSD_CONV_DIM), const),
                  pl.BlockSpec((1, LANES), const),
                  pl.BlockSpec((1, LANES), const),
                  pl.BlockSpec((1, LANES), const),
                  pl.BlockSpec((LANES, SSD_INNER), const),
                  pl.BlockSpec((LANES, SSD_INNER), const)],
        out_specs=[pl.BlockSpec((1, lv, SSD_INNER), row),
                   pl.BlockSpec((1, SSD_STATE, SSD_INNER), bat),
                   pl.BlockSpec((1, CONV_W - 1, SSD_CONV_DIM), bat)],
        out_shape=[jax.ShapeDtypeStruct((B, L, SSD_INNER), f32),
                   jax.ShapeDtypeStruct((B, SSD_STATE, SSD_INNER), f32),
                   jax.ShapeDtypeStruct((B, CONV_W - 1, SSD_CONV_DIM), f32)],
        scratch_shapes=[pltpu.VMEM((SUBLANES + lv, SSD_CONV_DIM), f32),
                        pltpu.VMEM((LANES, SSD_INNER), f32)],
        compiler_params=_params(("arbitrary", "arbitrary")),
        name="ssd",
    )(xbc, dt, st0, buf0, cw, cb, dtb, alog, dvec, expand, gmask)


def _lru_kernel(lx_ref, h0_ref, buf0_ref, cw_ref, cb_ref, wa_ref, ba_ref, wx_ref, bx_ref, lam_ref,
                hs_ref, hout_ref, buf_ref, xpad, a_s, u_s, hcar):
    c = pl.program_id(1)
    last = c == pl.num_programs(1) - 1
    lv = lx_ref.shape[1]

    @pl.when(c == 0)
    def _():
        hcar[0:1, :] = h0_ref[0]

    xb, hist = _conv_step(lx_ref, buf0_ref, cw_ref, cb_ref, xpad, c == 0)
    x16 = xb.astype(bf16)
    r = _sigmoid(jnp.dot(x16, wa_ref[...], preferred_element_type=f32) + ba_ref[...])
    i = _sigmoid(jnp.dot(x16, wx_ref[...], preferred_element_type=f32) + bx_ref[...])
    log_a = -LRU_C * r * _softplus(-lam_ref[...])
    a_s[...] = jnp.exp(log_a)
    u_s[...] = jnp.sqrt(1.0 - jnp.exp(2.0 * log_a)) * (i * xb)

    def body(j, h):
        r0 = pl.multiple_of(j * SUBLANES, SUBLANES)
        a8 = a_s[pl.ds(r0, SUBLANES), :]
        u8 = u_s[pl.ds(r0, SUBLANES), :]
        rows = []
        for k in range(SUBLANES):
            h = a8[k:k + 1, :] * h + u8[k:k + 1, :]
            rows.append(h)
        hs_ref[0, pl.ds(r0, SUBLANES), :] = jnp.concatenate(rows, axis=0)
        return h

    h = lax.fori_loop(0, lv // SUBLANES, body, hcar[0:1, :])
    hcar[0:1, :] = h

    @pl.when(last)
    def _():
        hout_ref[0] = h
        buf_ref[0] = hist


def _lru(lx, h0, buf0, cw, cb, wa, ba, wx, bx, lam):
    B, L, _ = lx.shape
    lv = min(L, SSD_CHUNK)
    assert L % lv == 0 and lv % SUBLANES == 0
    row = lambda b, c: (b, c, 0)
    bat = lambda b, c: (b, 0, 0)
    const = lambda b, c: (0, 0)
    W = LRU_WIDTH
    return pl.pallas_call(
        _lru_kernel,
        grid=(B, L // lv),
        in_specs=[pl.BlockSpec((1, lv, W), row),
                  pl.BlockSpec((1, 1, W), bat),
                  pl.BlockSpec((1, CONV_W - 1, W), bat),
                  pl.BlockSpec((CONV_W, W), const),
                  pl.BlockSpec((1, W), const),
                  pl.BlockSpec((W, W), const),
                  pl.BlockSpec((1, W), const),
                  pl.BlockSpec((W, W), const),
                  pl.BlockSpec((1, W), const),
                  pl.BlockSpec((1, W), const)],
        out_specs=[pl.BlockSpec((1, lv, W), row),
                   pl.BlockSpec((1, 1, W), bat),
                   pl.BlockSpec((1, CONV_W - 1, W), bat)],
        out_shape=[jax.ShapeDtypeStruct((B, L, W), f32),
                   jax.ShapeDtypeStruct((B, 1, W), f32),
                   jax.ShapeDtypeStruct((B, CONV_W - 1, W), f32)],
        scratch_shapes=[pltpu.VMEM((SUBLANES + lv, W), f32),
                        pltpu.VMEM((lv, W), f32),
                        pltpu.VMEM((lv, W), f32),
                        pltpu.VMEM((SUBLANES, W), f32)],
        compiler_params=_params(("arbitrary", "arbitrary")),
        name="lru",
    )(lx, h0, buf0, cw, cb, wa, ba, wx, bx, lam)


def _merge_kernel(y_ref, z_ref, hs_ref, ly_ref, gs_ref, gl_ref, x_ref, g1_ref, sc2_ref, sh2_ref,
                  nw_ref, ln2_ref, wbs_ref, wbl_ref, wo_ref, wq_ref, x1_ref, h2_ref, q_ref):
    bt, lt, _ = x_ref.shape
    rows = bt * lt
    flat = lambda v: v.reshape(rows, v.shape[-1])
    ys = _rms(y_ref[...] * _silu(z_ref[...]), nw_ref[...])
    yl = hs_ref[...] * _gelu(ly_ref[...])
    ps = jnp.dot(flat(ys).astype(bf16), wbs_ref[...], preferred_element_type=f32)
    plr = jnp.dot(flat(yl).astype(bf16), wbl_ref[...], preferred_element_type=f32)
    merged = _sigmoid(flat(gs_ref[...])) * ps + _sigmoid(flat(gl_ref[...])) * plr
    tm = jnp.dot(merged.astype(bf16), wo_ref[...], preferred_element_type=f32)
    x1 = x_ref[...] + g1_ref[...] * tm.reshape(bt, lt, D_MODEL)
    x1_ref[...] = x1
    h2 = _rms(x1, ln2_ref[...]) * (1.0 + sc2_ref[...]) + sh2_ref[...]
    h2_ref[...] = h2
    q = jnp.dot(flat(h2).astype(bf16), wq_ref[...], preferred_element_type=f32)
    for h in range(PEER_HEADS):
        q_ref[h] = q[:, h * PEER_KEY_DIM:(h + 1) * PEER_KEY_DIM].reshape(bt, lt, PEER_KEY_DIM)


def _merge(y, z, hs, ly, gs, gl, x, g1, sc2, sh2, nw, ln2, wbs, wbl, wo, wq):
    B, L, _ = x.shape
    bt, lt = _row_tiling(B, L)
    row = lambda i, j: (i, j, 0)
    bat = lambda i, j: (i, 0, 0)
    const = lambda i, j: (0, 0)
    act = pl.BlockSpec((bt, lt, D_MODEL), row)
    vec = pl.BlockSpec((bt, 1, D_MODEL), bat)
    return pl.pallas_call(
        _merge_kernel,
        grid=(B // bt, L // lt),
        in_specs=[act] * 7 + [vec] * 3
                 + [pl.BlockSpec((1, D_MODEL), const)] * 2
                 + [pl.BlockSpec((D_MODEL, D_MODEL), const)] * 3
                 + [pl.BlockSpec((D_MODEL, PEER_HEADS * PEER_KEY_DIM), const)],
        out_specs=[act, act,
                   pl.BlockSpec((PEER_HEADS, bt, lt, PEER_KEY_DIM), lambda i, j: (0, i, j, 0))],
        out_shape=[jax.ShapeDtypeStruct((B, L, D_MODEL), f32),
                   jax.ShapeDtypeStruct((B, L, D_MODEL), f32),
                   jax.ShapeDtypeStruct((PEER_HEADS, B, L, PEER_KEY_DIM), f32)],
        compiler_params=_params(("arbitrary", "arbitrary")),
        name="merge",
    )(y, z, hs, ly, gs, gl, x, g1, sc2, sh2, nw, ln2, wbs, wbl, wo, wq)


def _top16(s, idx):
    vals, ids = [], []
    for _ in range(PEER_TOPK):
        m = jnp.max(s, axis=0, keepdims=True)
        sel = jnp.min(jnp.where(s == m, idx, BIG), axis=0, keepdims=True)
        s = jnp.where(idx == sel, NEG, s)
        vals.append(m)
        ids.append(sel)
    return jnp.concatenate(vals, axis=0), jnp.concatenate(ids, axis=0)


def _topk_kernel(q_ref, k1_ref, k2_ref, e_ref, g_ref):
    tt = q_ref.shape[1]
    half = PEER_KEY_DIM // 2
    q = q_ref[0]
    nt = (((1,), (1,)), ((), ()))
    s1 = lax.dot_general(k1_ref[...], q[:, :half], nt, preferred_element_type=f32, precision=HI)
    s2 = lax.dot_general(k2_ref[...], q[:, half:], nt, preferred_element_type=f32, precision=HI)
    key = lax.broadcasted_iota(jnp.int32, (PEER_KEYS, tt), 0).astype(f32)
    v1, i1 = _top16(s1, key)
    v2, i2 = _top16(s2, key)
    cand = jnp.concatenate([v1[a:a + 1, :] + v2 for a in range(PEER_TOPK)], axis=0)
    eidx = jnp.concatenate([i1[a:a + 1, :] * float(PEER_KEYS) + i2 for a in range(PEER_TOPK)], axis=0)
    flat = lax.broadcasted_iota(jnp.int32, (PEER_TOPK * PEER_TOPK, tt), 0).astype(f32)
    svs, es = [], []
    s = cand
    for _ in range(PEER_TOPK):
        m = jnp.max(s, axis=0, keepdims=True)
        sel = jnp.min(jnp.where(s == m, flat, BIG), axis=0, keepdims=True)
        hit = flat == sel
        es.append(jnp.max(jnp.where(hit, eidx, -1.0), axis=0, keepdims=True))
        s = jnp.where(hit, NEG, s)
        svs.append(m)
    sv = jnp.concatenate(svs, axis=0)
    p = jnp.exp(sv - sv[0:1, :])
    g_ref[...] = p / jnp.sum(p, axis=0, keepdims=True)
    e_ref[...] = jnp.concatenate(es, axis=0).astype(jnp.int32)


def _topk(qh, k1, k2):
    H, T, _ = qh.shape
    tt = LANES
    assert T % tt == 0
    return pl.pallas_call(
        _topk_kernel,
        grid=(T // tt, H),
        in_specs=[pl.BlockSpec((1, tt, PEER_KEY_DIM), lambda i, h: (h, i, 0)),
                  pl.BlockSpec((PEER_KEYS, PEER_KEY_DIM // 2), lambda i, h: (0, 0)),
                  pl.BlockSpec((PEER_KEYS, PEER_KEY_DIM // 2), lambda i, h: (0, 0))],
        out_specs=[pl.BlockSpec((PEER_TOPK, tt), lambda i, h: (h, i)),
                   pl.BlockSpec((PEER_TOPK, tt), lambda i, h: (h, i))],
        out_shape=[jax.ShapeDtypeStruct((PEER_SEL, T), jnp.int32),
                   jax.ShapeDtypeStruct((PEER_SEL, T), f32)],
        compiler_params=_params(("arbitrary", "arbitrary")),
        name="topk",
    )(qh, k1, k2)


def _peer_kernel(h_ref, x1_ref, gt_ref, g2_ref, fw_ref, e_hbm, uv_hbm, o_ref,
                 buf0, buf1, esm, sem_uv, sem_e, *, rows_per_batch, final):
    tm = h_ref.shape[0]
    i = pl.program_id(0)
    ecp = pltpu.make_async_copy(e_hbm.at[pl.ds(i * (tm * PEER_SEL), tm * PEER_SEL)], esm, sem_e)
    ecp.start()
    ecp.wait()
    bufs = (buf0, buf1)

    def row_copy(t, k, slot):
        r0 = pl.multiple_of(esm[t * PEER_SEL + k] * UV_ROWS, UV_ROWS)
        return pltpu.make_async_copy(uv_hbm.at[pl.ds(r0, UV_ROWS), :],
                                     bufs[slot].at[pl.ds(k * UV_ROWS, UV_ROWS), :], sem_uv.at[slot])

    def issue(t, slot):
        for k in range(PEER_SEL):
            row_copy(t, k, slot).start(priority=k % 2)

    def wait(slot):
        pltpu.make_async_copy(uv_hbm.at[pl.ds(0, PEER_SEL * UV_ROWS), :], bufs[slot], sem_uv.at[slot]).wait()

    lane = lax.broadcasted_iota(jnp.int32, (PEER_SEL, tm), 1)
    nch = D_MODEL // LANES

    def compute(t, slot):
        buf = bufs[slot]
        hrow = h_ref[pl.ds(t, 1), :]
        words = lambda c: buf[pl.ds(c, PEER_SEL, stride=UV_ROWS), :]
        u_of = lambda w: lax.bitcast_convert_type(w & jnp.uint32(0xFFFF0000), f32)
        v_of = lambda w: lax.bitcast_convert_type(w << 16, f32)
        acc = u_of(words(0)) * hrow[:, 0:LANES]
        for c in range(1, nch):
            acc = acc + u_of(words(c)) * hrow[:, c * LANES:(c + 1) * LANES]
        s = jnp.sum(acc, axis=1, keepdims=True)
        gcol = jnp.sum(jnp.where(lane == t, gt_ref[...], 0.0), axis=1, keepdims=True)
        a = _gelu(s) * gcol
        outs = [jnp.sum(v_of(words(c)) * a, axis=0, keepdims=True) for c in range(nch)]
        o = jnp.concatenate(outs, axis=1)
        g2 = g2_ref[t // rows_per_batch] if rows_per_batch < tm else g2_ref[0]
        x2 = x1_ref[pl.ds(t, 1), :] + g2 * o
        if final:
            x2 = _rms(x2, fw_ref[...])
        o_ref[pl.ds(t, 1), :] = x2

    issue(0, 0)

    def pair(j, carry):
        t0 = 2 * j
        issue(t0 + 1, 1)
        wait(0)
        compute(t0, 0)

        @pl.when(t0 + 2 < tm)
        def _():
            issue(t0 + 2, 0)

        wait(1)
        compute(t0 + 1, 1)
        return carry

    lax.fori_loop(0, tm // 2, pair, 0)


def _peer(h2, x1, gt, g2, fw, e_flat, uv, rows_per_batch, final):
    T = h2.shape[0]
    tm = LANES
    assert T % tm == 0
    if rows_per_batch >= tm:
        assert rows_per_batch % tm == 0
        g2_spec = pl.BlockSpec((1, 1, D_MODEL), lambda i: (i * tm // rows_per_batch, 0, 0))
    else:
        assert tm % rows_per_batch == 0
        nb = tm // rows_per_batch
        g2_spec = pl.BlockSpec((nb, 1, D_MODEL), lambda i: (i, 0, 0))
    tok = pl.BlockSpec((tm, D_MODEL), lambda i: (i, 0))
    return pl.pallas_call(
        functools.partial(_peer_kernel, rows_per_batch=rows_per_batch, final=final),
        grid=(T // tm,),
        in_specs=[tok, tok,
                  pl.BlockSpec((PEER_SEL, tm), lambda i: (0, i)),
                  g2_spec,
                  pl.BlockSpec((1, D_MODEL), lambda i: (0, 0)),
                  pl.BlockSpec(memory_space=pl.ANY),
                  pl.BlockSpec(memory_space=pl.ANY)],
        out_specs=tok,
        out_shape=jax.ShapeDtypeStruct((T, D_MODEL), f32),
        scratch_shapes=[pltpu.VMEM((PEER_SEL * UV_ROWS, LANES), jnp.uint32),
                        pltpu.VMEM((PEER_SEL * UV_ROWS, LANES), jnp.uint32),
                        pltpu.SMEM((tm * PEER_SEL,), jnp.int32),
                        pltpu.SemaphoreType.DMA((2,)),
                        pltpu.SemaphoreType.DMA(())],
        compiler_params=_params(("arbitrary",)),
        name="peer",
    )(h2, x1, gt, g2, fw, e_flat, uv)


def _block_diag(w):
    n, d, _ = w.shape
    eye = jnp.eye(n, dtype=w.dtype)
    return (eye[:, None, :, None] * w[:, :, None, :]).reshape(n * d, n * d)


def _pad_lanes(v):
    return jnp.pad(v, (0, LANES - v.shape[0])).reshape(1, LANES)


def _prep_layer(l, p):
    w_in = p["w_in"][l]
    ws = [w_in[:, :Z_END], w_in[:, Z_END:XBC_END],
          jnp.pad(w_in[:, XBC_END:DT_END], ((0, 0), (0, LANES - SSD_HEADS))),
          w_in[:, DT_END:LX_END], w_in[:, LX_END:LY_END], w_in[:, LY_END:GS_END], w_in[:, GS_END:]]
    head = jnp.arange(SSD_INNER) // SSD_HEAD_DIM
    expand = (jnp.arange(LANES)[:, None] == head[None, :]).astype(f32)
    gmask = ((jnp.arange(LANES)[:, None] // SSD_STATE) == (head[None, :] // (SSD_HEADS // SSD_GROUPS))).astype(f32)
    bits = lambda t: lax.bitcast_convert_type(t.astype(bf16), jnp.uint16).astype(jnp.uint32)
    uv = (bits(p["peer_u"][l]) << 16) | bits(p["peer_v"][l])
    return dict(
        ws=[w.astype(bf16) for w in ws],
        ln1=p["ln1_w"][l].reshape(1, -1), ln2=p["ln2_w"][l].reshape(1, -1),
        ssd_cw=p["ssd_conv_w"][l], ssd_cb=p["ssd_conv_b"][l].reshape(1, -1),
        dtb=_pad_lanes(p["ssd_dt_bias"][l]), alog=_pad_lanes(p["ssd_a_log"][l]), dvec=_pad_lanes(p["ssd_d"][l]),
        expand=expand, gmask=gmask, nw=p["ssd_norm_w"][l].reshape(1, -1),
        lru_cw=p["lru_conv_w"][l], lru_cb=p["lru_conv_b"][l].reshape(1, -1),
        wa=_block_diag(p["lru_wa"][l]).astype(bf16), ba=p["lru_ba"][l].reshape(1, -1),
        wx=_block_diag(p["lru_wx"][l]).astype(bf16), bx=p["lru_bx"][l].reshape(1, -1),
        lam=p["lru_lambda"][l].reshape(1, -1),
        wbs=p["w_br_ssd"][l].astype(bf16), wbl=p["w_br_lru"][l].astype(bf16), wo=p["w_out"][l].astype(bf16),
        wq=p["peer_wq"][l].astype(bf16), k1=p["peer_k1"][l], k2=p["peer_k2"][l],
        uv=uv.reshape(PEER_EXPERTS * UV_ROWS, LANES),
    )


def _state_in(s):
    return s.transpose(0, 3, 1, 2).reshape(s.shape[0], SSD_STATE, SSD_INNER)


def _state_out(s):
    return s.reshape(s.shape[0], SSD_STATE, SSD_HEADS, SSD_HEAD_DIM).transpose(0, 2, 3, 1)


def _layer(x, mod, ssd_h0, ssd_buf, lru_h0, lru_buf, w, fw, final):
    B, L, _ = x.shape
    sh1, sc1, g1, sh2, sc2, g2 = [m.reshape(B, 1, D_MODEL) for m in jnp.split(mod, 6, axis=-1)]
    z, xbc, dt, lx, ly, gs, gl = _inproj(x, sc1, sh1, w["ln1"], w["ws"])
    y, st, sbuf = _ssd(xbc, dt, _state_in(ssd_h0), ssd_buf, w["ssd_cw"], w["ssd_cb"], w["dtb"], w["alog"],
                       w["dvec"], w["expand"], w["gmask"])
    hs, lh, lbuf = _lru(lx, lru_h0.reshape(B, 1, LRU_WIDTH), lru_buf, w["lru_cw"], w["lru_cb"], w["wa"], w["ba"],
                        w["wx"], w["bx"], w["lam"])
    x1, h2, qh = _merge(y, z, hs, ly, gs, gl, x, g1, sc2, sh2, w["nw"], w["ln2"], w["wbs"], w["wbl"], w["wo"],
                        w["wq"])
    T = B * L
    e_t, g_t = _topk(qh.reshape(PEER_HEADS, T, PEER_KEY_DIM), w["k1"], w["k2"])
    x2 = _peer(h2.reshape(T, D_MODEL), x1.reshape(T, D_MODEL), g_t, g2, fw, e_t.T.reshape(T * PEER_SEL), w["uv"],
               L, final)
    return x2.reshape(B, L, D_MODEL), _state_out(st), sbuf, lh.reshape(B, LRU_WIDTH), lbuf


def _trunk(x, c, ssd_h, ssd_buf, lru_h, lru_buf, p, layers):
    mod = _mod(c, p["w_mod"], p["b_mod"])
    fw = p["final_norm_w"].reshape(1, -1)
    outs = [[], [], [], []]
    for l in range(DEPTH):
        x, *new = _layer(x, mod[l], ssd_h[l], ssd_buf[l], lru_h[l], lru_buf[l], layers[l], fw, l == DEPTH - 1)
        for acc, v in zip(outs, new):
            acc.append(v)
    return (x,) + tuple(jnp.stack(o) for o in outs)


def kernel(x_prompt, x_sample, c_prompt, c_sample, state_ssd, state_ssd_conv, state_lru, state_lru_conv, ln1_w, ln2_w, w_mod, b_mod, w_in, ssd_conv_w, ssd_conv_b, ssd_dt_bias, ssd_a_log, ssd_d, ssd_norm_w, lru_conv_w, lru_conv_b, lru_wa, lru_ba, lru_wx, lru_bx, lru_lambda, w_br_ssd, w_br_lru, w_out, peer_wq, peer_k1, peer_k2, peer_u, peer_v, final_norm_w):
    p = dict(ln1_w=ln1_w, ln2_w=ln2_w, w_mod=w_mod, b_mod=b_mod, w_in=w_in, ssd_conv_w=ssd_conv_w,
             ssd_conv_b=ssd_conv_b, ssd_dt_bias=ssd_dt_bias, ssd_a_log=ssd_a_log, ssd_d=ssd_d,
             ssd_norm_w=ssd_norm_w, lru_conv_w=lru_conv_w, lru_conv_b=lru_conv_b, lru_wa=lru_wa, lru_ba=lru_ba,
             lru_wx=lru_wx, lru_bx=lru_bx, lru_lambda=lru_lambda, w_br_ssd=w_br_ssd, w_br_lru=w_br_lru,
             w_out=w_out, peer_wq=peer_wq, peer_k1=peer_k1, peer_k2=peer_k2, peer_u=peer_u, peer_v=peer_v,
             final_norm_w=final_norm_w)
    layers = [_prep_layer(l, p) for l in range(DEPTH)]
    bp = x_prompt.shape[0]
    zeros = lambda *s: jnp.zeros((DEPTH, bp) + s, f32)
    ys = _trunk(x_sample, c_sample, state_ssd, state_ssd_conv, state_lru, state_lru_conv, p, layers)
    yp = _trunk(x_prompt, c_prompt, zeros(SSD_HEADS, SSD_HEAD_DIM, SSD_STATE), zeros(CONV_W - 1, SSD_CONV_DIM),
                zeros(LRU_WIDTH), zeros(CONV_W - 1, LRU_WIDTH), p, layers)
    return (yp[0], ys[0]) + yp[1:] + ys[1:]
```

```python
import functools

import jax
import jax.numpy as jnp
from jax import lax
from jax.experimental import pallas as pl
from jax.experimental.pallas import tpu as pltpu

f32 = jnp.float32
bf16 = jnp.bfloat16
HI = lax.Precision.HIGHEST

D_MODEL = 1024
DEPTH = 2
CONV_W = 4
EPS = 1e-6
SSD_HEADS = 16
SSD_HEAD_DIM = 64
SSD_INNER = SSD_HEADS * SSD_HEAD_DIM
SSD_GROUPS = 2
SSD_STATE = 64
SSD_BC = SSD_GROUPS * SSD_STATE
SSD_CONV_DIM = SSD_INNER + 2 * SSD_BC
LRU_WIDTH = 1024
LRU_BLOCKS = 16
LRU_C = 8.0
PEER_HEADS = 8
PEER_KEYS = 128
PEER_EXPERTS = PEER_KEYS * PEER_KEYS
PEER_KEY_DIM = 256
PEER_TOPK = 16
PEER_SEL = PEER_HEADS * PEER_TOPK
Z_END = SSD_INNER
XBC_END = Z_END + SSD_CONV_DIM
DT_END = XBC_END + SSD_HEADS
LX_END = DT_END + LRU_WIDTH
LY_END = LX_END + LRU_WIDTH
GS_END = LY_END + D_MODEL

LANES = 128
SUBLANES = 8
SSD_CHUNK = 128
ROW_TILE = 256
UV_ROWS = D_MODEL // LANES
PEER_BUFS = 8
VMEM_LIMIT = 56 * 1024 * 1024
NEG = -3.0e38
BIG = 1.0e9
CAND_CUT = next(c for c in range(PEER_TOPK + 1) if (c + 1) ** 2 > PEER_TOPK)


def _sigmoid(x):
    return 1.0 / (1.0 + jnp.exp(-x))


def _silu(x):
    return x * _sigmoid(x)


def _gelu(x):
    return 0.5 * x * (1.0 + jnp.tanh(0.7978845608028654 * (x + 0.044715 * (x * x * x))))


def _softplus(x):
    return jnp.maximum(x, 0.0) + jnp.log(1.0 + jnp.exp(-jnp.abs(x)))


def _rms(x, w):
    return x * lax.rsqrt(jnp.mean(x * x, axis=-1, keepdims=True) + EPS) * w


def _params(sem):
    return pltpu.CompilerParams(dimension_semantics=sem, vmem_limit_bytes=VMEM_LIMIT)


def _row_tiling(B, L):
    if L >= ROW_TILE:
        assert L % ROW_TILE == 0
        return 1, ROW_TILE
    bt = min(B, ROW_TILE // L)
    assert B % bt == 0 and L % SUBLANES == 0
    return bt, L


def _mod_kernel(c_ref, w_ref, b_ref, o_ref):
    c = c_ref[...]
    o_ref[0] = jnp.dot(_silu(c), w_ref[0], preferred_element_type=f32, precision=HI) + b_ref[0]


def _mod(c, w_mod, b_mod):
    B = c.shape[0]
    n = w_mod.shape[-1]
    tn = 1536
    return pl.pallas_call(
        _mod_kernel,
        grid=(DEPTH, n // tn),
        in_specs=[pl.BlockSpec((B, D_MODEL), lambda l, j: (0, 0)),
                  pl.BlockSpec((1, D_MODEL, tn), lambda l, j: (l, 0, j)),
                  pl.BlockSpec((1, 1, tn), lambda l, j: (l, 0, j))],
        out_specs=pl.BlockSpec((1, B, tn), lambda l, j: (l, 0, j)),
        out_shape=jax.ShapeDtypeStruct((DEPTH, B, n), f32),
        compiler_params=_params(("arbitrary", "arbitrary")),
        name="mod",
    )(c, w_mod, b_mod.reshape(DEPTH, 1, n))


def _inproj_kernel(x_ref, sc_ref, sh_ref, lnw_ref, *refs):
    n = len(refs) // 2
    w_refs, o_refs = refs[:n], refs[n:]
    bt, lt, _ = x_ref.shape
    h = _rms(x_ref[...], lnw_ref[...]) * (1.0 + sc_ref[...]) + sh_ref[...]
    hb = h.reshape(bt * lt, D_MODEL).astype(bf16)
    for w_ref, o_ref in zip(w_refs, o_refs):
        o_ref[...] = jnp.dot(hb, w_ref[...], preferred_element_type=f32).reshape(o_ref.shape)


def _inproj(x, sc, sh, lnw, ws):
    B, L, _ = x.shape
    bt, lt = _row_tiling(B, L)
    row = lambda i, j: (i, j, 0)
    const = lambda i, j: (0, 0)
    return pl.pallas_call(
        _inproj_kernel,
        grid=(B // bt, L // lt),
        in_specs=[pl.BlockSpec((bt, lt, D_MODEL), row),
                  pl.BlockSpec((bt, 1, D_MODEL), lambda i, j: (i, 0, 0)),
                  pl.BlockSpec((bt, 1, D_MODEL), lambda i, j: (i, 0, 0)),
                  pl.BlockSpec((1, D_MODEL), const)]
                 + [pl.BlockSpec(w.shape, const) for w in ws],
        out_specs=[pl.BlockSpec((bt, lt, w.shape[1]), row) for w in ws],
        out_shape=[jax.ShapeDtypeStruct((B, L, w.shape[1]), f32) for w in ws],
        compiler_params=_params(("arbitrary", "arbitrary")),
        name="inproj",
    )(x, sc, sh, lnw, *ws)


def _conv_step(x_ref, buf0_ref, w_ref, b_ref, xpad, first):
    lv = x_ref.shape[1]

    @pl.when(first)
    def _():
        xpad[5:8, :] = buf0_ref[0]

    xpad[8:8 + lv, :] = x_ref[0]
    acc = b_ref[...] + w_ref[0:1, :] * xpad[5:5 + lv, :]
    for k in range(1, CONV_W):
        acc = acc + w_ref[k:k + 1, :] * xpad[5 + k:5 + k + lv, :]
    hist = xpad[5 + lv:8 + lv, :]
    xpad[5:8, :] = hist
    return acc, hist


def _pad_rows(x, rows):
    if x.shape[0] == rows:
        return x
    return jnp.concatenate([x, jnp.zeros((rows - x.shape[0], x.shape[1]), x.dtype)], axis=0)


def _ssd_kernel(xbc_ref, dt_ref, st0_ref, buf0_ref, cw_ref, cb_ref, dtb_ref, alog_ref, dvec_ref,
                expand_ref, gmask_ref, y_ref, st_ref, buf_ref, xpad, st):
    c = pl.program_id(1)
    last = c == pl.num_programs(1) - 1
    lv = xbc_ref.shape[1]
    Q = SSD_CHUNK

    @pl.when(c == 0)
    def _():
        s0 = st0_ref[0]
        st[...] = jnp.concatenate([s0, s0], axis=0) * gmask_ref[...]

    acc, hist = _conv_step(xbc_ref, buf0_ref, cw_ref, cb_ref, xpad, c == 0)
    xbc = _pad_rows(_silu(acc), Q)
    xs = xbc[:, :SSD_INNER]
    bm = xbc[:, SSD_INNER:SSD_INNER + SSD_BC]
    cm = xbc[:, SSD_INNER + SSD_BC:]

    lane = lax.broadcasted_iota(jnp.int32, (Q, LANES), 1)
    lane_v = lax.broadcasted_iota(jnp.int32, (lv, LANES), 1)
    dtv = _pad_rows(jnp.where(lane_v < SSD_HEADS, _softplus(dt_ref[0] + dtb_ref[...]), 0.0), Q)
    da = dtv * (-jnp.exp(alog_ref[...]))
    row = lax.broadcasted_iota(jnp.int32, (Q, Q), 0)
    col = lax.broadcasted_iota(jnp.int32, (Q, Q), 1)
    causal = row >= col
    acum = jnp.dot(causal.astype(f32), da, preferred_element_type=f32, precision=HI)
    acum_t = acum.T
    dt_t = dtv.T
    bm_t = bm.T
    alast = acum[Q - 1:Q, :]
    expand = expand_ref[...]
    ex = lambda v: jnp.dot(v, expand, preferred_element_type=f32, precision=HI)
    w_end = ex(jnp.exp(alast - acum) * dtv)
    e_in = ex(jnp.exp(acum))
    small = jnp.concatenate([jnp.exp(alast), dvec_ref[...], jnp.zeros((SUBLANES - 2, LANES), f32)], axis=0)
    small = ex(small)
    c_dec, d_full = small[0:1, :], small[1:2, :]

    st_in = st[...]
    y_off = jnp.dot(cm.astype(bf16), st_in.astype(bf16), preferred_element_type=f32) * e_in
    st_new = jnp.dot(bm_t.astype(bf16), (xs * w_end).astype(bf16), preferred_element_type=f32)
    st[...] = c_dec * st_in + st_new * gmask_ref[...]

    cb16, bm16 = cm.astype(bf16), bm.astype(bf16)
    nt = (((1,), (1,)), ((), ()))
    cbs = [lax.dot_general(jnp.where((lane >= g * SSD_STATE) & (lane < (g + 1) * SSD_STATE), cb16, 0), bm16, nt,
                           preferred_element_type=f32) for g in range(SSD_GROUPS)]
    hpg = SSD_HEADS // SSD_GROUPS
    ys = []
    for j in range(SSD_HEADS // 2):
        xp = xs[:, j * LANES:(j + 1) * LANES]
        wts, xsel = [], []
        for s in range(2):
            h = 2 * j + s
            seg = acum[:, h:h + 1] - acum_t[h:h + 1, :]
            dec = jnp.exp(jnp.where(causal, seg, NEG))
            wts.append((cbs[h // hpg] * dec * dt_t[h:h + 1, :]).astype(bf16))
            half = (lane >= s * SSD_HEAD_DIM) & (lane < (s + 1) * SSD_HEAD_DIM)
            xsel.append(jnp.where(half, xp, 0.0).astype(bf16))
        ys.append(jnp.dot(jnp.concatenate(wts, axis=1), jnp.concatenate(xsel, axis=0),
                          preferred_element_type=f32))
    y = jnp.concatenate(ys, axis=1) + y_off + d_full * xs
    y_ref[0] = y[:lv]

    @pl.when(last)
    def _():
        st_ref[0] = st[0:SSD_STATE, :] + st[SSD_STATE:, :]
        buf_ref[0] = hist


def _ssd(xbc, dt, st0, buf0, cw, cb, dtb, alog, dvec, expand, gmask):
    B, L, _ = xbc.shape
    lv = min(L, SSD_CHUNK)
    assert L % lv == 0
    row = lambda b, c: (b, c, 0)
    bat = lambda b, c: (b, 0, 0)
    const = lambda b, c: (0, 0)
    return pl.pallas_call(
        _ssd_kernel,
        grid=(B, L // lv),
        in_specs=[pl.BlockSpec((1, lv, SSD_CONV_DIM), row),
                  pl.BlockSpec((1, lv, LANES), row),
                  pl.BlockSpec((1, SSD_STATE, SSD_INNER), bat),
                  pl.BlockSpec((1, CONV_W - 1, SSD_CONV_DIM), bat),
                  pl.BlockSpec((CONV_W, SSD_CONV_DIM), const),
                  pl.BlockSpec((1, SSD_CONV_DIM), const),
                  pl.BlockSpec((1, LANES), const),
                  pl.BlockSpec((1, LANES), const),
                  pl.BlockSpec((1, LANES), const),
                  pl.BlockSpec((LANES, SSD_INNER), const),
                  pl.BlockSpec((LANES, SSD_INNER), const)],
        out_specs=[pl.BlockSpec((1, lv, SSD_INNER), row),
                   pl.BlockSpec((1, SSD_STATE, SSD_INNER), bat),
                   pl.BlockSpec((1, CONV_W - 1, SSD_CONV_DIM), bat)],
        out_shape=[jax.ShapeDtypeStruct((B, L, SSD_INNER), f32),
                   jax.ShapeDtypeStruct((B, SSD_STATE, SSD_INNER), f32),
                   jax.ShapeDtypeStruct((B, CONV_W - 1, SSD_CONV_DIM), f32)],
        scratch_shapes=[pltpu.VMEM((SUBLANES + lv, SSD_CONV_DIM), f32),
                        pltpu.VMEM((LANES, SSD_INNER), f32)],
        compiler_params=_params(("arbitrary", "arbitrary")),
        name="ssd",
    )(xbc, dt, st0, buf0, cw, cb, dtb, alog, dvec, expand, gmask)


def _lru_kernel(lx_ref, h0_ref, buf0_ref, cw_ref, cb_ref, wa_ref, ba_ref, wx_ref, bx_ref, lam_ref,
                hs_ref, hout_ref, buf_ref, xpad, a_s, u_s, hcar):
    c = pl.program_id(1)
    last = c == pl.num_programs(1) - 1
    lv = lx_ref.shape[1]

    @pl.when(c == 0)
    def _():
        hcar[0:1, :] = h0_ref[0]

    xb, hist = _conv_step(lx_ref, buf0_ref, cw_ref, cb_ref, xpad, c == 0)
    x16 = xb.astype(bf16)
    r = _sigmoid(jnp.dot(x16, wa_ref[...], preferred_element_type=f32) + ba_ref[...])
    i = _sigmoid(jnp.dot(x16, wx_ref[...], preferred_element_type=f32) + bx_ref[...])
    log_a = -LRU_C * r * _softplus(-lam_ref[...])
    a_s[...] = jnp.exp(log_a)
    u_s[...] = jnp.sqrt(1.0 - jnp.exp(2.0 * log_a)) * (i * xb)

    def body(j, h):
        r0 = pl.multiple_of(j * SUBLANES, SUBLANES)
        a8 = a_s[pl.ds(r0, SUBLANES), :]
        u8 = u_s[pl.ds(r0, SUBLANES), :]
        rows = []
        for k in range(SUBLANES):
            h = a8[k:k + 1, :] * h + u8[k:k + 1, :]
            rows.append(h)
        hs_ref[0, pl.ds(r0, SUBLANES), :] = jnp.concatenate(rows, axis=0)
        return h

    h = lax.fori_loop(0, lv // SUBLANES, body, hcar[0:1, :])
    hcar[0:1, :] = h

    @pl.when(last)
    def _():
        hout_ref[0] = h
        buf_ref[0] = hist


def _lru(lx, h0, buf0, cw, cb, wa, ba, wx, bx, lam):
    B, L, _ = lx.shape
    lv = min(L, SSD_CHUNK)
    assert L % lv == 0 and lv % SUBLANES == 0
    row = lambda b, c: (b, c, 0)
    bat = lambda b, c: (b, 0, 0)
    const = lambda b, c: (0, 0)
    W = LRU_WIDTH
    return pl.pallas_call(
        _lru_kernel,
        grid=(B, L // lv),
        in_specs=[pl.BlockSpec((1, lv, W), row),
                  pl.BlockSpec((1, 1, W), bat),
                  pl.BlockSpec((1, CONV_W - 1, W), bat),
                  pl.BlockSpec((CONV_W, W), const),
                  pl.BlockSpec((1, W), const),
                  pl.BlockSpec((W, W), const),
                  pl.BlockSpec((1, W), const),
                  pl.BlockSpec((W, W), const),
                  pl.BlockSpec((1, W), const),
                  pl.BlockSpec((1, W), const)],
        out_specs=[pl.BlockSpec((1, lv, W), row),
                   pl.BlockSpec((1, 1, W), bat),
                   pl.BlockSpec((1, CONV_W - 1, W), bat)],
        out_shape=[jax.ShapeDtypeStruct((B, L, W), f32),
                   jax.ShapeDtypeStruct((B, 1, W), f32),
                   jax.ShapeDtypeStruct((B, CONV_W - 1, W), f32)],
        scratch_shapes=[pltpu.VMEM((SUBLANES + lv, W), f32),
                        pltpu.VMEM((lv, W), f32),
                        pltpu.VMEM((lv, W), f32),
                        pltpu.VMEM((SUBLANES, W), f32)],
        compiler_params=_params(("arbitrary", "arbitrary")),
        name="lru",
    )(lx, h0, buf0, cw, cb, wa, ba, wx, bx, lam)


def _merge_kernel(y_ref, z_ref, hs_ref, ly_ref, gs_ref, gl_ref, x_ref, g1_ref, sc2_ref, sh2_ref,
                  nw_ref, ln2_ref, wbs_ref, wbl_ref, wo_ref, wq_ref, x1_ref, h2_ref, q_ref):
    bt, lt, _ = x_ref.shape
    rows = bt * lt
    flat = lambda v: v.reshape(rows, v.shape[-1])
    ys = _rms(y_ref[...] * _silu(z_ref[...]), nw_ref[...])
    yl = hs_ref[...] * _gelu(ly_ref[...])
    ps = jnp.dot(flat(ys).astype(bf16), wbs_ref[...], preferred_element_type=f32)
    plr = jnp.dot(flat(yl).astype(bf16), wbl_ref[...], preferred_element_type=f32)
    merged = _sigmoid(flat(gs_ref[...])) * ps + _sigmoid(flat(gl_ref[...])) * plr
    tm = jnp.dot(merged.astype(bf16), wo_ref[...], preferred_element_type=f32)
    x1 = x_ref[...] + g1_ref[...] * tm.reshape(bt, lt, D_MODEL)
    x1_ref[...] = x1
    h2 = _rms(x1, ln2_ref[...]) * (1.0 + sc2_ref[...]) + sh2_ref[...]
    h2_ref[...] = h2
    q = jnp.dot(flat(h2).astype(bf16), wq_ref[...], preferred_element_type=f32)
    for h in range(PEER_HEADS):
        q_ref[h] = q[:, h * PEER_KEY_DIM:(h + 1) * PEER_KEY_DIM].reshape(bt, lt, PEER_KEY_DIM)


def _merge(y, z, hs, ly, gs, gl, x, g1, sc2, sh2, nw, ln2, wbs, wbl, wo, wq):
    B, L, _ = x.shape
    bt, lt = _row_tiling(B, L)
    row = lambda i, j: (i, j, 0)
    bat = lambda i, j: (i, 0, 0)
    const = lambda i, j: (0, 0)
    act = pl.BlockSpec((bt, lt, D_MODEL), row)
    vec = pl.BlockSpec((bt, 1, D_MODEL), bat)
    return pl.pallas_call(
        _merge_kernel,
        grid=(B // bt, L // lt),
        in_specs=[act] * 7 + [vec] * 3
                 + [pl.BlockSpec((1, D_MODEL), const)] * 2
                 + [pl.BlockSpec((D_MODEL, D_MODEL), const)] * 3
                 + [pl.BlockSpec((D_MODEL, PEER_HEADS * PEER_KEY_DIM), const)],
        out_specs=[act, act,
                   pl.BlockSpec((PEER_HEADS, bt, lt, PEER_KEY_DIM), lambda i, j: (0, i, j, 0))],
        out_shape=[jax.ShapeDtypeStruct((B, L, D_MODEL), f32),
                   jax.ShapeDtypeStruct((B, L, D_MODEL), f32),
                   jax.ShapeDtypeStruct((PEER_HEADS, B, L, PEER_KEY_DIM), f32)],
        compiler_params=_params(("arbitrary", "arbitrary")),
        name="merge",
    )(y, z, hs, ly, gs, gl, x, g1, sc2, sh2, nw, ln2, wbs, wbl, wo, wq)


def _top16(s, idx):
    vals, ids = [], []
    for _ in range(PEER_TOPK):
        m = jnp.max(s, axis=0, keepdims=True)
        sel = jnp.min(jnp.where(s == m, idx, BIG), axis=0, keepdims=True)
        s = jnp.where(idx == sel, NEG, s)
        vals.append(m)
        ids.append(sel)
    return jnp.concatenate(vals, axis=0), jnp.concatenate(ids, axis=0)


def _topk_kernel(q_ref, k1_ref, k2_ref, e_ref, g_ref):
    tt = q_ref.shape[1]
    half = PEER_KEY_DIM // 2
    q = q_ref[0]
    nt = (((1,), (1,)), ((), ()))
    s1 = lax.dot_general(k1_ref[...], q[:, :half], nt, preferred_element_type=f32, precision=HI)
    s2 = lax.dot_general(k2_ref[...], q[:, half:], nt, preferred_element_type=f32, precision=HI)
    key = lax.broadcasted_iota(jnp.int32, (PEER_KEYS, tt), 0).astype(f32)
    v1, i1 = _top16(s1, key)
    v2, i2 = _top16(s2, key)
    rank = lax.broadcasted_iota(jnp.int32, (PEER_TOPK, tt), 0).astype(f32)
    tail = rank >= float(CAND_CUT)
    cand = jnp.concatenate([v1[a:a + 1, :] + v2 for a in range(CAND_CUT)]
                           + [jnp.where(tail, v1 + v2[b:b + 1, :], NEG) for b in range(CAND_CUT)], axis=0)
    eidx = jnp.concatenate([i1[a:a + 1, :] * float(PEER_KEYS) + i2 for a in range(CAND_CUT)]
                           + [i1 * float(PEER_KEYS) + i2[b:b + 1, :] for b in range(CAND_CUT)], axis=0)
    flat = jnp.concatenate([rank + float(a * PEER_TOPK) for a in range(CAND_CUT)]
                           + [rank * float(PEER_TOPK) + float(b) for b in range(CAND_CUT)], axis=0)
    svs, es = [], []
    s = cand
    for _ in range(PEER_TOPK):
        m = jnp.max(s, axis=0, keepdims=True)
        sel = jnp.min(jnp.where(s == m, flat, BIG), axis=0, keepdims=True)
        hit = flat == sel
        es.append(jnp.max(jnp.where(hit, eidx, -1.0), axis=0, keepdims=True))
        s = jnp.where(hit, NEG, s)
        svs.append(m)
    sv = jnp.concatenate(svs, axis=0)
    p = jnp.exp(sv - sv[0:1, :])
    g_ref[...] = p / jnp.sum(p, axis=0, keepdims=True)
    e_ref[...] = jnp.concatenate(es, axis=0).astype(jnp.int32)


def _topk(qh, k1, k2):
    H, T, _ = qh.shape
    tt = LANES
    assert T % tt == 0
    return pl.pallas_call(
        _topk_kernel,
        grid=(T // tt, H),
        in_specs=[pl.BlockSpec((1, tt, PEER_KEY_DIM), lambda i, h: (h, i, 0)),
                  pl.BlockSpec((PEER_KEYS, PEER_KEY_DIM // 2), lambda i, h: (0, 0)),
                  pl.BlockSpec((PEER_KEYS, PEER_KEY_DIM // 2), lambda i, h: (0, 0))],
        out_specs=[pl.BlockSpec((PEER_TOPK, tt), lambda i, h: (h, i)),
                   pl.BlockSpec((PEER_TOPK, tt), lambda i, h: (h, i))],
        out_shape=[jax.ShapeDtypeStruct((PEER_SEL, T), jnp.int32),
                   jax.ShapeDtypeStruct((PEER_SEL, T), f32)],
        compiler_params=_params(("arbitrary", "arbitrary")),
        name="topk",
    )(qh, k1, k2)


def _peer_kernel(h_ref, x1_ref, gt_ref, g2_ref, fw_ref, e_hbm, uv_hbm, o_ref, *scratch, rows_per_batch, final):
    bufs = scratch[:PEER_BUFS]
    esm, sem_uv, sem_e = scratch[PEER_BUFS:]
    tm = h_ref.shape[0]
    i = pl.program_id(0)
    ecp = pltpu.make_async_copy(e_hbm.at[pl.ds(i * (tm * PEER_SEL), tm * PEER_SEL)], esm, sem_e)
    ecp.start()
    ecp.wait()

    def push(t, slot, ks):
        for k in ks:
            r0 = pl.multiple_of(esm[t * PEER_SEL + k] * UV_ROWS, UV_ROWS)
            pltpu.make_async_copy(uv_hbm.at[pl.ds(r0, UV_ROWS), :], bufs[slot].at[pl.ds(k * UV_ROWS, UV_ROWS), :],
                                  sem_uv.at[slot]).start(priority=k % 2)

    def wait(slot):
        pltpu.make_async_copy(uv_hbm.at[pl.ds(0, PEER_SEL * UV_ROWS), :], bufs[slot], sem_uv.at[slot]).wait()

    lane = lax.broadcasted_iota(jnp.int32, (PEER_SEL, tm), 1)
    nch = D_MODEL // LANES
    per = PEER_SEL // (2 * nch)

    def token(t, slot, nxt, nslot):
        buf = bufs[slot]
        hrow = h_ref[pl.ds(t, 1), :]
        words = lambda c: buf[pl.ds(c, PEER_SEL, stride=UV_ROWS), :]
        u_of = lambda w: lax.bitcast_convert_type(w & jnp.uint32(0xFFFF0000), f32)
        v_of = lambda w: lax.bitcast_convert_type(w << 16, f32)
        acc = None
        for c in range(nch):
            if nxt is not None:
                push(nxt, nslot, range(c * per, (c + 1) * per))
            p = u_of(words(c)) * hrow[:, c * LANES:(c + 1) * LANES]
            acc = p if acc is None else acc + p
        s = jnp.sum(acc, axis=1, keepdims=True)
        gcol = jnp.sum(jnp.where(lane == t, gt_ref[...], 0.0), axis=1, keepdims=True)
        a = _gelu(s) * gcol
        outs = []
        for c in range(nch):
            if nxt is not None:
                push(nxt, nslot, range((nch + c) * per, (nch + c + 1) * per))
            outs.append(jnp.sum(v_of(words(c)) * a, axis=0, keepdims=True))
        o = jnp.concatenate(outs, axis=1)
        g2 = g2_ref[t // rows_per_batch] if rows_per_batch < tm else g2_ref[0]
        x2 = x1_ref[pl.ds(t, 1), :] + g2 * o
        if final:
            x2 = _rms(x2, fw_ref[...])
        o_ref[pl.ds(t, 1), :] = x2

    ahead = PEER_BUFS - 1
    for s in range(ahead):
        push(s, s, range(PEER_SEL))

    def group(j, carry):
        t0 = PEER_BUFS * j
        for s in range(PEER_BUFS):
            wait(s)
            token(t0 + s, s, t0 + s + ahead, (s + ahead) % PEER_BUFS)
        return carry

    lax.fori_loop(0, tm // PEER_BUFS - 1, group, 0)
    t0 = tm - PEER_BUFS
    for s in range(PEER_BUFS):
        wait(s)
        token(t0 + s, s, t0 + s + ahead if s == 0 else None, (s + ahead) % PEER_BUFS)


def _peer(h2, x1, gt, g2, fw, e_flat, uv, rows_per_batch, final):
    T = h2.shape[0]
    tm = LANES
    assert T % tm == 0
    if rows_per_batch >= tm:
        assert rows_per_batch % tm == 0
        g2_spec = pl.BlockSpec((1, 1, D_MODEL), lambda i: (i * tm // rows_per_batch, 0, 0))
    else:
        assert tm % rows_per_batch == 0
        nb = tm // rows_per_batch
        g2_spec = pl.BlockSpec((nb, 1, D_MODEL), lambda i: (i, 0, 0))
    tok = pl.BlockSpec((tm, D_MODEL), lambda i: (i, 0))
    return pl.pallas_call(
        functools.partial(_peer_kernel, rows_per_batch=rows_per_batch, final=final),
        grid=(T // tm,),
        in_specs=[tok, tok,
                  pl.BlockSpec((PEER_SEL, tm), lambda i: (0, i)),
                  g2_spec,
                  pl.BlockSpec((1, D_MODEL), lambda i: (0, 0)),
                  pl.BlockSpec(memory_space=pl.ANY),
                  pl.BlockSpec(memory_space=pl.ANY)],
        out_specs=tok,
        out_shape=jax.ShapeDtypeStruct((T, D_MODEL), f32),
        scratch_shapes=[pltpu.VMEM((PEER_SEL * UV_ROWS, LANES), jnp.uint32)] * PEER_BUFS
                       + [pltpu.SMEM((tm * PEER_SEL,), jnp.int32),
                          pltpu.SemaphoreType.DMA((PEER_BUFS,)),
                          pltpu.SemaphoreType.DMA(())],
        compiler_params=_params(("arbitrary",)),
        name="peer",
    )(h2, x1, gt, g2, fw, e_flat, uv)


def _block_diag(w):
    n, d, _ = w.shape
    eye = jnp.eye(n, dtype=w.dtype)
    return (eye[:, None, :, None] * w[:, :, None, :]).reshape(n * d, n * d)


def _pad_lanes(v):
    return jnp.pad(v, (0, LANES - v.shape[0])).reshape(1, LANES)


def _prep_layer(l, p):
    w_in = p["w_in"][l]
    ws = [w_in[:, :Z_END], w_in[:, Z_END:XBC_END],
          jnp.pad(w_in[:, XBC_END:DT_END], ((0, 0), (0, LANES - SSD_HEADS))),
          w_in[:, DT_END:LX_END], w_in[:, LX_END:LY_END], w_in[:, LY_END:GS_END], w_in[:, GS_END:]]
    head = jnp.arange(SSD_INNER) // SSD_HEAD_DIM
    expand = (jnp.arange(LANES)[:, None] == head[None, :]).astype(f32)
    gmask = ((jnp.arange(LANES)[:, None] // SSD_STATE) == (head[None, :] // (SSD_HEADS // SSD_GROUPS))).astype(f32)
    bits = lambda t: lax.bitcast_convert_type(t.astype(bf16), jnp.uint16).astype(jnp.uint32)
    uv = (bits(p["peer_u"][l]) << 16) | bits(p["peer_v"][l])
    return dict(
        ws=[w.astype(bf16) for w in ws],
        ln1=p["ln1_w"][l].reshape(1, -1), ln2=p["ln2_w"][l].reshape(1, -1),
        ssd_cw=p["ssd_conv_w"][l], ssd_cb=p["ssd_conv_b"][l].reshape(1, -1),
        dtb=_pad_lanes(p["ssd_dt_bias"][l]), alog=_pad_lanes(p["ssd_a_log"][l]), dvec=_pad_lanes(p["ssd_d"][l]),
        expand=expand, gmask=gmask, nw=p["ssd_norm_w"][l].reshape(1, -1),
        lru_cw=p["lru_conv_w"][l], lru_cb=p["lru_conv_b"][l].reshape(1, -1),
        wa=_block_diag(p["lru_wa"][l]).astype(bf16), ba=p["lru_ba"][l].reshape(1, -1),
        wx=_block_diag(p["lru_wx"][l]).astype(bf16), bx=p["lru_bx"][l].reshape(1, -1),
        lam=p["lru_lambda"][l].reshape(1, -1),
        wbs=p["w_br_ssd"][l].astype(bf16), wbl=p["w_br_lru"][l].astype(bf16), wo=p["w_out"][l].astype(bf16),
        wq=p["peer_wq"][l].astype(bf16), k1=p["peer_k1"][l], k2=p["peer_k2"][l],
        uv=uv.reshape(PEER_EXPERTS * UV_ROWS, LANES),
    )


def _state_in(s):
    return s.transpose(0, 3, 1, 2).reshape(s.shape[0], SSD_STATE, SSD_INNER)


def _state_out(s):
    return s.reshape(s.shape[0], SSD_STATE, SSD_HEADS, SSD_HEAD_DIM).transpose(0, 2, 3, 1)


def _layer(x, mod, ssd_h0, ssd_buf, lru_h0, lru_buf, w, fw, final):
    B, L, _ = x.shape
    sh1, sc1, g1, sh2, sc2, g2 = [m.reshape(B, 1, D_MODEL) for m in jnp.split(mod, 6, axis=-1)]
    z, xbc, dt, lx, ly, gs, gl = _inproj(x, sc1, sh1, w["ln1"], w["ws"])
    y, st, sbuf = _ssd(xbc, dt, _state_in(ssd_h0), ssd_buf, w["ssd_cw"], w["ssd_cb"], w["dtb"], w["alog"],
                       w["dvec"], w["expand"], w["gmask"])
    hs, lh, lbuf = _lru(lx, lru_h0.reshape(B, 1, LRU_WIDTH), lru_buf, w["lru_cw"], w["lru_cb"], w["wa"], w["ba"],
                        w["wx"], w["bx"], w["lam"])
    x1, h2, qh = _merge(y, z, hs, ly, gs, gl, x, g1, sc2, sh2, w["nw"], w["ln2"], w["wbs"], w["wbl"], w["wo"],
                        w["wq"])
    T = B * L
    e_t, g_t = _topk(qh.reshape(PEER_HEADS, T, PEER_KEY_DIM), w["k1"], w["k2"])
    x2 = _peer(h2.reshape(T, D_MODEL), x1.reshape(T, D_MODEL), g_t, g2, fw, e_t.T.reshape(T * PEER_SEL), w["uv"],
               L, final)
    return x2.reshape(B, L, D_MODEL), _state_out(st), sbuf, lh.reshape(B, LRU_WIDTH), lbuf


def _trunk(x, c, ssd_h, ssd_buf, lru_h, lru_buf, p, layers):
    mod = _mod(c, p["w_mod"], p["b_mod"])
    fw = p["final_norm_w"].reshape(1, -1)
    outs = [[], [], [], []]
    for l in range(DEPTH):
        x, *new = _layer(x, mod[l], ssd_h[l], ssd_buf[l], lru_h[l], lru_buf[l], layers[l], fw, l == DEPTH - 1)
        for acc, v in zip(outs, new):
            acc.append(v)
    return (x,) + tuple(jnp.stack(o) for o in outs)


def kernel(x_prompt, x_sample, c_prompt, c_sample, state_ssd, state_ssd_conv, state_lru, state_lru_conv, ln1_w, ln2_w, w_mod, b_mod, w_in, ssd_conv_w, ssd_conv_b, ssd_dt_bias, ssd_a_log, ssd_d, ssd_norm_w, lru_conv_w, lru_conv_b, lru_wa, lru_ba, lru_wx, lru_bx, lru_lambda, w_br_ssd, w_br_lru, w_out, peer_wq, peer_k1, peer_k2, peer_u, peer_v, final_norm_w):
    p = dict(ln1_w=ln1_w, ln2_w=ln2_w, w_mod=w_mod, b_mod=b_mod, w_in=w_in, ssd_conv_w=ssd_conv_w,
             ssd_conv_b=ssd_conv_b, ssd_dt_bias=ssd_dt_bias, ssd_a_log=ssd_a_log, ssd_d=ssd_d,
             ssd_norm_w=ssd_norm_w, lru_conv_w=lru_conv_w, lru_conv_b=lru_conv_b, lru_wa=lru_wa, lru_ba=lru_ba,
             lru_wx=lru_wx, lru_bx=lru_bx, lru_lambda=lru_lambda, w_br_ssd=w_br_ssd, w_br_lru=w_br_lru,
             w_out=w_out, peer_wq=peer_wq, peer_k1=peer_k1, peer_k2=peer_k2, peer_u=peer_u, peer_v=peer_v,
             final_norm_w=final_norm_w)
    layers = [_prep_layer(l, p) for l in range(DEPTH)]
    bp = x_prompt.shape[0]
    zeros = lambda *s: jnp.zeros((DEPTH, bp) + s, f32)
    ys = _trunk(x_sample, c_sample, state_ssd, state_ssd_conv, state_lru, state_lru_conv, p, layers)
    yp = _trunk(x_prompt, c_prompt, zeros(SSD_HEADS, SSD_HEAD_DIM, SSD_STATE), zeros(CONV_W - 1, SSD_CONV_DIM),
                zeros(LRU_WIDTH), zeros(CONV_W - 1, LRU_WIDTH), p, layers)
    return (yp[0], ys[0]) + yp[1:] + ys[1:]
```

```python
import functools

import jax
import jax.numpy as jnp
from jax import lax
from jax.experimental import pallas as pl
from jax.experimental.pallas import tpu as pltpu

f32 = jnp.float32
bf16 = jnp.bfloat16
HI = lax.Precision.HIGHEST

D_MODEL = 1024
DEPTH = 2
CONV_W = 4
EPS = 1e-6
SSD_HEADS = 16
SSD_HEAD_DIM = 64
SSD_INNER = SSD_HEADS * SSD_HEAD_DIM
SSD_GROUPS = 2
SSD_STATE = 64
SSD_BC = SSD_GROUPS * SSD_STATE
SSD_CONV_DIM = SSD_INNER + 2 * SSD_BC
LRU_WIDTH = 1024
LRU_BLOCKS = 16
LRU_C = 8.0
PEER_HEADS = 8
PEER_KEYS = 128
PEER_EXPERTS = PEER_KEYS * PEER_KEYS
PEER_KEY_DIM = 256
PEER_TOPK = 16
PEER_SEL = PEER_HEADS * PEER_TOPK
Z_END = SSD_INNER
XBC_END = Z_END + SSD_CONV_DIM
DT_END = XBC_END + SSD_HEADS
LX_END = DT_END + LRU_WIDTH
LY_END = LX_END + LRU_WIDTH
GS_END = LY_END + D_MODEL

LANES = 128
SUBLANES = 8
SSD_CHUNK = 128
ROW_TILE = 256
UV_ROWS = D_MODEL // LANES
PEER_BUFS = 8
VMEM_LIMIT = 56 * 1024 * 1024
NEG = -3.0e38
BIG = 1.0e9
CAND_CUT = next(c for c in range(PEER_TOPK + 1) if (c + 1) ** 2 > PEER_TOPK)


def _sigmoid(x):
    return 1.0 / (1.0 + jnp.exp(-x))


def _silu(x):
    return x * _sigmoid(x)


def _gelu(x):
    return 0.5 * x * (1.0 + jnp.tanh(0.7978845608028654 * (x + 0.044715 * (x * x * x))))


def _softplus(x):
    return jnp.maximum(x, 0.0) + jnp.log(1.0 + jnp.exp(-jnp.abs(x)))


def _rms(x, w):
    return x * lax.rsqrt(jnp.mean(x * x, axis=-1, keepdims=True) + EPS) * w


def _params(sem):
    return pltpu.CompilerParams(dimension_semantics=sem, vmem_limit_bytes=VMEM_LIMIT)


def _row_tiling(B, L):
    if L >= ROW_TILE:
        assert L % ROW_TILE == 0
        return 1, ROW_TILE
    bt = min(B, ROW_TILE // L)
    assert B % bt == 0 and L % SUBLANES == 0
    return bt, L


def _mod_kernel(c_ref, w_ref, b_ref, o_ref):
    c = c_ref[...]
    o_ref[0] = jnp.dot(_silu(c), w_ref[0], preferred_element_type=f32, precision=HI) + b_ref[0]


def _mod(c, w_mod, b_mod):
    B = c.shape[0]
    n = w_mod.shape[-1]
    tn = 1536
    return pl.pallas_call(
        _mod_kernel,
        grid=(DEPTH, n // tn),
        in_specs=[pl.BlockSpec((B, D_MODEL), lambda l, j: (0, 0)),
                  pl.BlockSpec((1, D_MODEL, tn), lambda l, j: (l, 0, j)),
                  pl.BlockSpec((1, 1, tn), lambda l, j: (l, 0, j))],
        out_specs=pl.BlockSpec((1, B, tn), lambda l, j: (l, 0, j)),
        out_shape=jax.ShapeDtypeStruct((DEPTH, B, n), f32),
        compiler_params=_params(("arbitrary", "arbitrary")),
        name="mod",
    )(c, w_mod, b_mod.reshape(DEPTH, 1, n))


def _inproj_kernel(x_ref, sc_ref, sh_ref, lnw_ref, *refs):
    n = len(refs) // 2
    w_refs, o_refs = refs[:n], refs[n:]
    bt, lt, _ = x_ref.shape
    h = _rms(x_ref[...], lnw_ref[...]) * (1.0 + sc_ref[...]) + sh_ref[...]
    hb = h.reshape(bt * lt, D_MODEL).astype(bf16)
    for w_ref, o_ref in zip(w_refs, o_refs):
        o_ref[...] = jnp.dot(hb, w_ref[...], preferred_element_type=f32).reshape(o_ref.shape)


def _inproj(x, sc, sh, lnw, ws):
    B, L, _ = x.shape
    bt, lt = _row_tiling(B, L)
    row = lambda i, j: (i, j, 0)
    const = lambda i, j: (0, 0)
    return pl.pallas_call(
        _inproj_kernel,
        grid=(B // bt, L // lt),
        in_specs=[pl.BlockSpec((bt, lt, D_MODEL), row),
                  pl.BlockSpec((bt, 1, D_MODEL), lambda i, j: (i, 0, 0)),
                  pl.BlockSpec((bt, 1, D_MODEL), lambda i, j: (i, 0, 0)),
                  pl.BlockSpec((1, D_MODEL), const)]
                 + [pl.BlockSpec(w.shape, const) for w in ws],
        out_specs=[pl.BlockSpec((bt, lt, w.shape[1]), row) for w in ws],
        out_shape=[jax.ShapeDtypeStruct((B, L, w.shape[1]), f32) for w in ws],
        compiler_params=_params(("arbitrary", "arbitrary")),
        name="inproj",
    )(x, sc, sh, lnw, *ws)


def _conv_step(x_ref, buf0_ref, w_ref, b_ref, xpad, first):
    lv = x_ref.shape[1]

    @pl.when(first)
    def _():
        xpad[5:8, :] = buf0_ref[0]

    xpad[8:8 + lv, :] = x_ref[0]
    acc = b_ref[...] + w_ref[0:1, :] * xpad[5:5 + lv, :]
    for k in range(1, CONV_W):
        acc = acc + w_ref[k:k + 1, :] * xpad[5 + k:5 + k + lv, :]
    hist = xpad[5 + lv:8 + lv, :]
    xpad[5:8, :] = hist
    return acc, hist


def _pad_rows(x, rows):
    if x.shape[0] == rows:
        return x
    return jnp.concatenate([x, jnp.zeros((rows - x.shape[0], x.shape[1]), x.dtype)], axis=0)


def _ssd_kernel(xbc_ref, dt_ref, st0_ref, buf0_ref, cw_ref, cb_ref, dtb_ref, alog_ref, dvec_ref,
                expand_ref, gmask_ref, y_ref, st_ref, buf_ref, xpad, st):
    c = pl.program_id(1)
    last = c == pl.num_programs(1) - 1
    lv = xbc_ref.shape[1]
    Q = SSD_CHUNK

    @pl.when(c == 0)
    def _():
        s0 = st0_ref[0]
        st[...] = jnp.concatenate([s0, s0], axis=0) * gmask_ref[...]

    acc, hist = _conv_step(xbc_ref, buf0_ref, cw_ref, cb_ref, xpad, c == 0)
    xbc = _pad_rows(_silu(acc), Q)
    xs = xbc[:, :SSD_INNER]
    bm = xbc[:, SSD_INNER:SSD_INNER + SSD_BC]
    cm = xbc[:, SSD_INNER + SSD_BC:]

    lane = lax.broadcasted_iota(jnp.int32, (Q, LANES), 1)
    lane_v = lax.broadcasted_iota(jnp.int32, (lv, LANES), 1)
    dtv = _pad_rows(jnp.where(lane_v < SSD_HEADS, _softplus(dt_ref[0] + dtb_ref[...]), 0.0), Q)
    da = dtv * (-jnp.exp(alog_ref[...]))
    row = lax.broadcasted_iota(jnp.int32, (Q, Q), 0)
    col = lax.broadcasted_iota(jnp.int32, (Q, Q), 1)
    causal = row >= col
    acum = jnp.dot(causal.astype(f32), da, preferred_element_type=f32, precision=HI)
    acum_t = acum.T
    dt_t = dtv.T
    bm_t = bm.T
    alast = acum[Q - 1:Q, :]
    expand = expand_ref[...]
    ex = lambda v: jnp.dot(v, expand, preferred_element_type=f32, precision=HI)
    w_end = ex(jnp.exp(alast - acum) * dtv)
    e_in = ex(jnp.exp(acum))
    small = jnp.concatenate([jnp.exp(alast), dvec_ref[...], jnp.zeros((SUBLANES - 2, LANES), f32)], axis=0)
    small = ex(small)
    c_dec, d_full = small[0:1, :], small[1:2, :]

    st_in = st[...]
    y_off = jnp.dot(cm.astype(bf16), st_in.astype(bf16), preferred_element_type=f32) * e_in
    st_new = jnp.dot(bm_t.astype(bf16), (xs * w_end).astype(bf16), preferred_element_type=f32)
    st[...] = c_dec * st_in + st_new * gmask_ref[...]

    cb16, bm16 = cm.astype(bf16), bm.astype(bf16)
    nt = (((1,), (1,)), ((), ()))
    cbs = [lax.dot_general(jnp.where((lane >= g * SSD_STATE) & (lane < (g + 1) * SSD_STATE), cb16, 0), bm16, nt,
                           preferred_element_type=f32) for g in range(SSD_GROUPS)]
    hpg = SSD_HEADS // SSD_GROUPS
    ys = []
    for j in range(SSD_HEADS // 2):
        xp = xs[:, j * LANES:(j + 1) * LANES]
        wts, xsel = [], []
        for s in range(2):
            h = 2 * j + s
            seg = acum[:, h:h + 1] - acum_t[h:h + 1, :]
            dec = jnp.exp(jnp.where(causal, seg, NEG))
            wts.append((cbs[h // hpg] * dec * dt_t[h:h + 1, :]).astype(bf16))
            half = (lane >= s * SSD_HEAD_DIM) & (lane < (s + 1) * SSD_HEAD_DIM)
            xsel.append(jnp.where(half, xp, 0.0).astype(bf16))
        ys.append(jnp.dot(jnp.concatenate(wts, axis=1), jnp.concatenate(xsel, axis=0),
                          preferred_element_type=f32))
    y = jnp.concatenate(ys, axis=1) + y_off + d_full * xs
    y_ref[0] = y[:lv]

    @pl.when(last)
    def _():
        st_ref[0] = st[0:SSD_STATE, :] + st[SSD_STATE:, :]
        buf_ref[0] = hist


def _ssd(xbc, dt, st0, buf0, cw, cb, dtb, alog, dvec, expand, gmask):
    B, L, _ = xbc.shape
    lv = min(L, SSD_CHUNK)
    assert L % lv == 0
    row = lambda b, c: (b, c, 0)
    bat = lambda b, c: (b, 0, 0)
    const = lambda b, c: (0, 0)
    return pl.pallas_call(
        _ssd_kernel,
        grid=(B, L // lv),
        in_specs=[pl.BlockSpec((1, lv, SSD_CONV_DIM), row),
                  pl.BlockSpec((1, lv, LANES), row),
                  pl.BlockSpec((1, SSD_STATE, SSD_INNER), bat),
                  pl.BlockSpec((1, CONV_W - 1, SSD_CONV_DIM), bat),
                  pl.BlockSpec((CONV_W, SSD_CONV_DIM), const),
                  pl.BlockSpec((1, SSD_CONV_DIM), const),
                  pl.BlockSpec((1, LANES), const),
                  pl.BlockSpec((1, LANES), const),
                  pl.BlockSpec((1, LANES), const),
                  pl.BlockSpec((LANES, SSD_INNER), const),
                  pl.BlockSpec((LANES, SSD_INNER), const)],
        out_specs=[pl.BlockSpec((1, lv, SSD_INNER), row),
                   pl.BlockSpec((1, SSD_STATE, SSD_INNER), bat),
                   pl.BlockSpec((1, CONV_W - 1, SSD_CONV_DIM), bat)],
        out_shape=[jax.ShapeDtypeStruct((B, L, SSD_INNER), f32),
                   jax.ShapeDtypeStruct((B, SSD_STATE, SSD_INNER), f32),
                   jax.ShapeDtypeStruct((B, CONV_W - 1, SSD_CONV_DIM), f32)],
        scratch_shapes=[pltpu.VMEM((SUBLANES + lv, SSD_CONV_DIM), f32),
                        pltpu.VMEM((LANES, SSD_INNER), f32)],
        compiler_params=_params(("arbitrary", "arbitrary")),
        name="ssd",
    )(xbc, dt, st0, buf0, cw, cb, dtb, alog, dvec, expand, gmask)


def _lru_kernel(lx_ref, h0_ref, buf0_ref, cw_ref, cb_ref, wa_ref, ba_ref, wx_ref, bx_ref, lam_ref,
                hs_ref, hout_ref, buf_ref, xpad, a_s, u_s, hcar):
    c = pl.program_id(1)
    last = c == pl.num_programs(1) - 1
    lv = lx_ref.shape[1]

    @pl.when(c == 0)
    def _():
        hcar[0:1, :] = h0_ref[0]

    xb, hist = _conv_step(lx_ref, buf0_ref, cw_ref, cb_ref, xpad, c == 0)
    x16 = xb.astype(bf16)
    r = _sigmoid(jnp.dot(x16, wa_ref[...], preferred_element_type=f32) + ba_ref[...])
    i = _sigmoid(jnp.dot(x16, wx_ref[...], preferred_element_type=f32) + bx_ref[...])
    log_a = -LRU_C * r * _softplus(-lam_ref[...])
    a_s[...] = jnp.exp(log_a)
    u_s[...] = jnp.sqrt(1.0 - jnp.exp(2.0 * log_a)) * (i * xb)

    def body(j, h):
        r0 = pl.multiple_of(j * SUBLANES, SUBLANES)
        a8 = a_s[pl.ds(r0, SUBLANES), :]
        u8 = u_s[pl.ds(r0, SUBLANES), :]
        rows = []
        for k in range(SUBLANES):
            h = a8[k:k + 1, :] * h + u8[k:k + 1, :]
            rows.append(h)
        hs_ref[0, pl.ds(r0, SUBLANES), :] = jnp.concatenate(rows, axis=0)
        return h

    h = lax.fori_loop(0, lv // SUBLANES, body, hcar[0:1, :])
    hcar[0:1, :] = h

    @pl.when(last)
    def _():
        hout_ref[0] = h
        buf_ref[0] = hist


def _lru(lx, h0, buf0, cw, cb, wa, ba, wx, bx, lam):
    B, L, _ = lx.shape
    lv = min(L, SSD_CHUNK)
    assert L % lv == 0 and lv % SUBLANES == 0
    row = lambda b, c: (b, c, 0)
    bat = lambda b, c: (b, 0, 0)
    const = lambda b, c: (0, 0)
    W = LRU_WIDTH
    return pl.pallas_call(
        _lru_kernel,
        grid=(B, L // lv),
        in_specs=[pl.BlockSpec((1, lv, W), row),
                  pl.BlockSpec((1, 1, W), bat),
                  pl.BlockSpec((1, CONV_W - 1, W), bat),
                  pl.BlockSpec((CONV_W, W), const),
                  pl.BlockSpec((1, W), const),
                  pl.BlockSpec((W, W), const),
                  pl.BlockSpec((1, W), const),
                  pl.BlockSpec((W, W), const),
                  pl.BlockSpec((1, W), const),
                  pl.BlockSpec((1, W), const)],
        out_specs=[pl.BlockSpec((1, lv, W), row),
                   pl.BlockSpec((1, 1, W), bat),
                   pl.BlockSpec((1, CONV_W - 1, W), bat)],
        out_shape=[jax.ShapeDtypeStruct((B, L, W), f32),
                   jax.ShapeDtypeStruct((B, 1, W), f32),
                   jax.ShapeDtypeStruct((B, CONV_W - 1, W), f32)],
        scratch_shapes=[pltpu.VMEM((SUBLANES + lv, W), f32),
                        pltpu.VMEM((lv, W), f32),
                        pltpu.VMEM((lv, W), f32),
                        pltpu.VMEM((SUBLANES, W), f32)],
        compiler_params=_params(("arbitrary", "arbitrary")),
        name="lru",
    )(lx, h0, buf0, cw, cb, wa, ba, wx, bx, lam)


def _merge_kernel(y_ref, z_ref, hs_ref, ly_ref, gs_ref, gl_ref, x_ref, g1_ref, sc2_ref, sh2_ref,
                  nw_ref, ln2_ref, wbs_ref, wbl_ref, wo_ref, wq_ref, x1_ref, h2_ref, q_ref):
    bt, lt, _ = x_ref.shape
    rows = bt * lt
    flat = lambda v: v.reshape(rows, v.shape[-1])
    ys = _rms(y_ref[...] * _silu(z_ref[...]), nw_ref[...])
    yl = hs_ref[...] * _gelu(ly_ref[...])
    ps = jnp.dot(flat(ys).astype(bf16), wbs_ref[...], preferred_element_type=f32)
    plr = jnp.dot(flat(yl).astype(bf16), wbl_ref[...], preferred_element_type=f32)
    merged = _sigmoid(flat(gs_ref[...])) * ps + _sigmoid(flat(gl_ref[...])) * plr
    tm = jnp.dot(merged.astype(bf16), wo_ref[...], preferred_element_type=f32)
    x1 = x_ref[...] + g1_ref[...] * tm.reshape(bt, lt, D_MODEL)
    x1_ref[...] = x1
    h2 = _rms(x1, ln2_ref[...]) * (1.0 + sc2_ref[...]) + sh2_ref[...]
    h2_ref[...] = h2
    q = jnp.dot(flat(h2).astype(bf16), wq_ref[...], preferred_element_type=f32)
    for h in range(PEER_HEADS):
        q_ref[h] = q[:, h * PEER_KEY_DIM:(h + 1) * PEER_KEY_DIM].reshape(bt, lt, PEER_KEY_DIM)


def _merge(y, z, hs, ly, gs, gl, x, g1, sc2, sh2, nw, ln2, wbs, wbl, wo, wq):
    B, L, _ = x.shape
    bt, lt = _row_tiling(B, L)
    row = lambda i, j: (i, j, 0)
    bat = lambda i, j: (i, 0, 0)
    const = lambda i, j: (0, 0)
    act = pl.BlockSpec((bt, lt, D_MODEL), row)
    vec = pl.BlockSpec((bt, 1, D_MODEL), bat)
    return pl.pallas_call(
        _merge_kernel,
        grid=(B // bt, L // lt),
        in_specs=[act] * 7 + [vec] * 3
                 + [pl.BlockSpec((1, D_MODEL), const)] * 2
                 + [pl.BlockSpec((D_MODEL, D_MODEL), const)] * 3
                 + [pl.BlockSpec((D_MODEL, PEER_HEADS * PEER_KEY_DIM), const)],
        out_specs=[act, act,
                   pl.BlockSpec((PEER_HEADS, bt, lt, PEER_KEY_DIM), lambda i, j: (0, i, j, 0))],
        out_shape=[jax.ShapeDtypeStruct((B, L, D_MODEL), f32),
                   jax.ShapeDtypeStruct((B, L, D_MODEL), f32),
                   jax.ShapeDtypeStruct((PEER_HEADS, B, L, PEER_KEY_DIM), f32)],
        compiler_params=_params(("arbitrary", "arbitrary")),
        name="merge",
    )(y, z, hs, ly, gs, gl, x, g1, sc2, sh2, nw, ln2, wbs, wbl, wo, wq)


def _top16(s, idx):
    vals, ids = [], []
    for _ in range(PEER_TOPK):
        m = jnp.max(s, axis=0, keepdims=True)
        sel = jnp.min(jnp.where(s == m, idx, BIG), axis=0, keepdims=True)
        s = jnp.where(idx == sel, NEG, s)
        vals.append(m)
        ids.append(sel)
    return jnp.concatenate(vals, axis=0), jnp.concatenate(ids, axis=0)


def _topk_kernel(q_ref, k1_ref, k2_ref, e_ref, g_ref):
    tt = q_ref.shape[1]
    half = PEER_KEY_DIM // 2
    q = q_ref[0]
    nt = (((1,), (1,)), ((), ()))
    s1 = lax.dot_general(k1_ref[...], q[:, :half], nt, preferred_element_type=f32, precision=HI)
    s2 = lax.dot_general(k2_ref[...], q[:, half:], nt, preferred_element_type=f32, precision=HI)
    key = lax.broadcasted_iota(jnp.int32, (PEER_KEYS, tt), 0).astype(f32)
    v1, i1 = _top16(s1, key)
    v2, i2 = _top16(s2, key)
    rank = lax.broadcasted_iota(jnp.int32, (PEER_TOPK, tt), 0).astype(f32)
    tail = rank >= float(CAND_CUT)
    cand = jnp.concatenate([v1[a:a + 1, :] + v2 for a in range(CAND_CUT)]
                           + [jnp.where(tail, v1 + v2[b:b + 1, :], NEG) for b in range(CAND_CUT)], axis=0)
    eidx = jnp.concatenate([i1[a:a + 1, :] * float(PEER_KEYS) + i2 for a in range(CAND_CUT)]
                           + [i1 * float(PEER_KEYS) + i2[b:b + 1, :] for b in range(CAND_CUT)], axis=0)
    flat = jnp.concatenate([rank + float(a * PEER_TOPK) for a in range(CAND_CUT)]
                           + [rank * float(PEER_TOPK) + float(b) for b in range(CAND_CUT)], axis=0)
    svs, es = [], []
    s = cand
    for _ in range(PEER_TOPK):
        m = jnp.max(s, axis=0, keepdims=True)
        sel = jnp.min(jnp.where(s == m, flat, BIG), axis=0, keepdims=True)
        hit = flat == sel
        es.append(jnp.max(jnp.where(hit, eidx, -1.0), axis=0, keepdims=True))
        s = jnp.where(hit, NEG, s)
        svs.append(m)
    sv = jnp.concatenate(svs, axis=0)
    p = jnp.exp(sv - sv[0:1, :])
    g_ref[...] = p / jnp.sum(p, axis=0, keepdims=True)
    e_ref[...] = jnp.concatenate(es, axis=0).astype(jnp.int32)


def _topk(qh, k1, k2):
    H, T, _ = qh.shape
    tt = LANES
    assert T % tt == 0
    return pl.pallas_call(
        _topk_kernel,
        grid=(T // tt, H),
        in_specs=[pl.BlockSpec((1, tt, PEER_KEY_DIM), lambda i, h: (h, i, 0)),
                  pl.BlockSpec((PEER_KEYS, PEER_KEY_DIM // 2), lambda i, h: (0, 0)),
                  pl.BlockSpec((PEER_KEYS, PEER_KEY_DIM // 2), lambda i, h: (0, 0))],
        out_specs=[pl.BlockSpec((PEER_TOPK, tt), lambda i, h: (h, i)),
                   pl.BlockSpec((PEER_TOPK, tt), lambda i, h: (h, i))],
        out_shape=[jax.ShapeDtypeStruct((PEER_SEL, T), jnp.int32),
                   jax.ShapeDtypeStruct((PEER_SEL, T), f32)],
        compiler_params=_params(("arbitrary", "arbitrary")),
        name="topk",
    )(qh, k1, k2)


def _peer_kernel(h_ref, x1_ref, gt_ref, g2_ref, fw_ref, e_hbm, uv_hbm, o_ref, *scratch, rows_per_batch, final):
    bufs = scratch[:PEER_BUFS]
    esm, sem_uv, sem_e = scratch[PEER_BUFS:]
    tm = h_ref.shape[0]
    i = pl.program_id(0)
    ecp = pltpu.make_async_copy(e_hbm.at[pl.ds(i * (tm * PEER_SEL), tm * PEER_SEL)], esm, sem_e)
    ecp.start()
    ecp.wait()

    def push(t, slot, ks):
        for k in ks:
            r0 = pl.multiple_of(esm[t * PEER_SEL + k] * UV_ROWS, UV_ROWS)
            pltpu.make_async_copy(uv_hbm.at[pl.ds(r0, UV_ROWS), :], bufs[slot].at[pl.ds(k * UV_ROWS, UV_ROWS), :],
                                  sem_uv.at[slot]).start(priority=k % 2)

    def wait(slot):
        pltpu.make_async_copy(uv_hbm.at[pl.ds(0, PEER_SEL * UV_ROWS), :], bufs[slot], sem_uv.at[slot]).wait()

    lane = lax.broadcasted_iota(jnp.int32, (PEER_SEL, tm), 1)
    nch = D_MODEL // LANES
    per = PEER_SEL // (2 * nch)

    def token(t, slot, nxt, nslot):
        buf = bufs[slot]
        hrow = h_ref[pl.ds(t, 1), :]
        words = lambda c: buf[pl.ds(c, PEER_SEL, stride=UV_ROWS), :]
        u_of = lambda w: lax.bitcast_convert_type(w & jnp.uint32(0xFFFF0000), f32)
        v_of = lambda w: lax.bitcast_convert_type(w << 16, f32)
        acc = None
        for c in range(nch):
            if nxt is not None:
                push(nxt, nslot, range(c * per, (c + 1) * per))
            p = u_of(words(c)) * hrow[:, c * LANES:(c + 1) * LANES]
            acc = p if acc is None else acc + p
        s = jnp.sum(acc, axis=1, keepdims=True)
        gcol = jnp.sum(jnp.where(lane == t, gt_ref[...], 0.0), axis=1, keepdims=True)
        a = _gelu(s) * gcol
        outs = []
        for c in range(nch):
            if nxt is not None:
                push(nxt, nslot, range((nch + c) * per, (nch + c + 1) * per))
            outs.append(jnp.sum(v_of(words(c)) * a, axis=0, keepdims=True))
        o = jnp.concatenate(outs, axis=1)
        g2 = g2_ref[t // rows_per_batch] if rows_per_batch < tm else g2_ref[0]
        o_ref[pl.ds(t, 1), :] = x1_ref[pl.ds(t, 1), :] + g2 * o

    ahead = PEER_BUFS - 1
    for s in range(ahead):
        push(s, s, range(PEER_SEL))

    def group(j, carry):
        t0 = PEER_BUFS * j
        for s in range(PEER_BUFS):
            wait(s)
            token(t0 + s, s, t0 + s + ahead, (s + ahead) % PEER_BUFS)
        return carry

    lax.fori_loop(0, tm // PEER_BUFS - 1, group, 0)
    t0 = tm - PEER_BUFS
    for s in range(PEER_BUFS):
        wait(s)
        token(t0 + s, s, t0 + s + ahead if s == 0 else None, (s + ahead) % PEER_BUFS)
    if final:
        o_ref[...] = _rms(o_ref[...], fw_ref[...])


def _peer(h2, x1, gt, g2, fw, e_flat, uv, rows_per_batch, final):
    T = h2.shape[0]
    tm = LANES
    assert T % tm == 0
    if rows_per_batch >= tm:
        assert rows_per_batch % tm == 0
        g2_spec = pl.BlockSpec((1, 1, D_MODEL), lambda i: (i * tm // rows_per_batch, 0, 0))
    else:
        assert tm % rows_per_batch == 0
        nb = tm // rows_per_batch
        g2_spec = pl.BlockSpec((nb, 1, D_MODEL), lambda i: (i, 0, 0))
    tok = pl.BlockSpec((tm, D_MODEL), lambda i: (i, 0))
    return pl.pallas_call(
        functools.partial(_peer_kernel, rows_per_batch=rows_per_batch, final=final),
        grid=(T // tm,),
        in_specs=[tok, tok,
                  pl.BlockSpec((PEER_SEL, tm), lambda i: (0, i)),
                  g2_spec,
                  pl.BlockSpec((1, D_MODEL), lambda i: (0, 0)),
                  pl.BlockSpec(memory_space=pl.ANY),
                  pl.BlockSpec(memory_space=pl.ANY)],
        out_specs=tok,
        out_shape=jax.ShapeDtypeStruct((T, D_MODEL), f32),
        scratch_shapes=[pltpu.VMEM((PEER_SEL * UV_ROWS, LANES), jnp.uint32)] * PEER_BUFS
                       + [pltpu.SMEM((tm * PEER_SEL,), jnp.int32),
                          pltpu.SemaphoreType.DMA((PEER_BUFS,)),
                          pltpu.SemaphoreType.DMA(())],
        compiler_params=_params(("arbitrary",)),
        name="peer",
    )(h2, x1, gt, g2, fw, e_flat, uv)


def _block_diag(w):
    n, d, _ = w.shape
    eye = jnp.eye(n, dtype=w.dtype)
    return (eye[:, None, :, None] * w[:, :, None, :]).reshape(n * d, n * d)


def _pad_lanes(v):
    return jnp.pad(v, (0, LANES - v.shape[0])).reshape(1, LANES)


def _prep_layer(l, p):
    w_in = p["w_in"][l]
    ws = [w_in[:, :Z_END], w_in[:, Z_END:XBC_END],
          jnp.pad(w_in[:, XBC_END:DT_END], ((0, 0), (0, LANES - SSD_HEADS))),
          w_in[:, DT_END:LX_END], w_in[:, LX_END:LY_END], w_in[:, LY_END:GS_END], w_in[:, GS_END:]]
    head = jnp.arange(SSD_INNER) // SSD_HEAD_DIM
    expand = (jnp.arange(LANES)[:, None] == head[None, :]).astype(f32)
    gmask = ((jnp.arange(LANES)[:, None] // SSD_STATE) == (head[None, :] // (SSD_HEADS // SSD_GROUPS))).astype(f32)
    bits = lambda t: lax.bitcast_convert_type(t.astype(bf16), jnp.uint16).astype(jnp.uint32)
    uv = (bits(p["peer_u"][l]) << 16) | bits(p["peer_v"][l])
    return dict(
        ws=[w.astype(bf16) for w in ws],
        ln1=p["ln1_w"][l].reshape(1, -1), ln2=p["ln2_w"][l].reshape(1, -1),
        ssd_cw=p["ssd_conv_w"][l], ssd_cb=p["ssd_conv_b"][l].reshape(1, -1),
        dtb=_pad_lanes(p["ssd_dt_bias"][l]), alog=_pad_lanes(p["ssd_a_log"][l]), dvec=_pad_lanes(p["ssd_d"][l]),
        expand=expand, gmask=gmask, nw=p["ssd_norm_w"][l].reshape(1, -1),
        lru_cw=p["lru_conv_w"][l], lru_cb=p["lru_conv_b"][l].reshape(1, -1),
        wa=_block_diag(p["lru_wa"][l]).astype(bf16), ba=p["lru_ba"][l].reshape(1, -1),
        wx=_block_diag(p["lru_wx"][l]).astype(bf16), bx=p["lru_bx"][l].reshape(1, -1),
        lam=p["lru_lambda"][l].reshape(1, -1),
        wbs=p["w_br_ssd"][l].astype(bf16), wbl=p["w_br_lru"][l].astype(bf16), wo=p["w_out"][l].astype(bf16),
        wq=p["peer_wq"][l].astype(bf16), k1=p["peer_k1"][l], k2=p["peer_k2"][l],
        uv=uv.reshape(PEER_EXPERTS * UV_ROWS, LANES),
    )


def _state_in(s):
    return s.transpose(0, 3, 1, 2).reshape(s.shape[0], SSD_STATE, SSD_INNER)


def _state_out(s):
    return s.reshape(s.shape[0], SSD_STATE, SSD_HEADS, SSD_HEAD_DIM).transpose(0, 2, 3, 1)


def _layer(x, mod, ssd_h0, ssd_buf, lru_h0, lru_buf, w, fw, final):
    B, L, _ = x.shape
    sh1, sc1, g1, sh2, sc2, g2 = [m.reshape(B, 1, D_MODEL) for m in jnp.split(mod, 6, axis=-1)]
    z, xbc, dt, lx, ly, gs, gl = _inproj(x, sc1, sh1, w["ln1"], w["ws"])
    y, st, sbuf = _ssd(xbc, dt, _state_in(ssd_h0), ssd_buf, w["ssd_cw"], w["ssd_cb"], w["dtb"], w["alog"],
                       w["dvec"], w["expand"], w["gmask"])
    hs, lh, lbuf = _lru(lx, lru_h0.reshape(B, 1, LRU_WIDTH), lru_buf, w["lru_cw"], w["lru_cb"], w["wa"], w["ba"],
                        w["wx"], w["bx"], w["lam"])
    x1, h2, qh = _merge(y, z, hs, ly, gs, gl, x, g1, sc2, sh2, w["nw"], w["ln2"], w["wbs"], w["wbl"], w["wo"],
                        w["wq"])
    T = B * L
    e_t, g_t = _topk(qh.reshape(PEER_HEADS, T, PEER_KEY_DIM), w["k1"], w["k2"])
    x2 = _peer(h2.reshape(T, D_MODEL), x1.reshape(T, D_MODEL), g_t, g2, fw, e_t.T.reshape(T * PEER_SEL), w["uv"],
               L, final)
    return x2.reshape(B, L, D_MODEL), _state_out(st), sbuf, lh.reshape(B, LRU_WIDTH), lbuf


def _trunk(x, c, ssd_h, ssd_buf, lru_h, lru_buf, p, layers):
    mod = _mod(c, p["w_mod"], p["b_mod"])
    fw = p["final_norm_w"].reshape(1, -1)
    outs = [[], [], [], []]
    for l in range(DEPTH):
        x, *new = _layer(x, mod[l], ssd_h[l], ssd_buf[l], lru_h[l], lru_buf[l], layers[l], fw, l == DEPTH - 1)
        for acc, v in zip(outs, new):
            acc.append(v)
    return (x,) + tuple(jnp.stack(o) for o in outs)


def kernel(x_prompt, x_sample, c_prompt, c_sample, state_ssd, state_ssd_conv, state_lru, state_lru_conv, ln1_w, ln2_w, w_mod, b_mod, w_in, ssd_conv_w, ssd_conv_b, ssd_dt_bias, ssd_a_log, ssd_d, ssd_norm_w, lru_conv_w, lru_conv_b, lru_wa, lru_ba, lru_wx, lru_bx, lru_lambda, w_br_ssd, w_br_lru, w_out, peer_wq, peer_k1, peer_k2, peer_u, peer_v, final_norm_w):
    p = dict(ln1_w=ln1_w, ln2_w=ln2_w, w_mod=w_mod, b_mod=b_mod, w_in=w_in, ssd_conv_w=ssd_conv_w,
             ssd_conv_b=ssd_conv_b, ssd_dt_bias=ssd_dt_bias, ssd_a_log=ssd_a_log, ssd_d=ssd_d,
             ssd_norm_w=ssd_norm_w, lru_conv_w=lru_conv_w, lru_conv_b=lru_conv_b, lru_wa=lru_wa, lru_ba=lru_ba,
             lru_wx=lru_wx, lru_bx=lru_bx, lru_lambda=lru_lambda, w_br_ssd=w_br_ssd, w_br_lru=w_br_lru,
             w_out=w_out, peer_wq=peer_wq, peer_k1=peer_k1, peer_k2=peer_k2, peer_u=peer_u, peer_v=peer_v,
             final_norm_w=final_norm_w)
    layers = [_prep_layer(l, p) for l in range(DEPTH)]
    bp = x_prompt.shape[0]
    zeros = lambda *s: jnp.zeros((DEPTH, bp) + s, f32)
    ys = _trunk(x_sample, c_sample, state_ssd, state_ssd_conv, state_lru, state_lru_conv, p, layers)
    yp = _trunk(x_prompt, c_prompt, zeros(SSD_HEADS, SSD_HEAD_DIM, SSD_STATE), zeros(CONV_W - 1, SSD_CONV_DIM),
                zeros(LRU_WIDTH), zeros(CONV_W - 1, LRU_WIDTH), p, layers)
    return (yp[0], ys[0]) + yp[1:] + ys[1:]
```

```python
import functools

import jax
import jax.numpy as jnp
from jax import lax
from jax.experimental import pallas as pl
from jax.experimental.pallas import tpu as pltpu

f32 = jnp.float32
bf16 = jnp.bfloat16
HI = lax.Precision.HIGHEST

D_MODEL = 1024
DEPTH = 2
CONV_W = 4
EPS = 1e-6
SSD_HEADS = 16
SSD_HEAD_DIM = 64
SSD_INNER = SSD_HEADS * SSD_HEAD_DIM
SSD_GROUPS = 2
SSD_STATE = 64
SSD_BC = SSD_GROUPS * SSD_STATE
SSD_CONV_DIM = SSD_INNER + 2 * SSD_BC
LRU_WIDTH = 1024
LRU_BLOCKS = 16
LRU_C = 8.0
PEER_HEADS = 8
PEER_KEYS = 128
PEER_EXPERTS = PEER_KEYS * PEER_KEYS
PEER_KEY_DIM = 256
PEER_TOPK = 16
PEER_SEL = PEER_HEADS * PEER_TOPK
Z_END = SSD_INNER
XBC_END = Z_END + SSD_CONV_DIM
DT_END = XBC_END + SSD_HEADS
LX_END = DT_END + LRU_WIDTH
LY_END = LX_END + LRU_WIDTH
GS_END = LY_END + D_MODEL

LANES = 128
SUBLANES = 8
SSD_CHUNK = 128
ROW_TILE = 256
UV_ROWS = D_MODEL // LANES
TOPK_TILE = 512
PEER_BUFS = 8
VMEM_LIMIT = 56 * 1024 * 1024
NEG = -3.0e38
BIG = 1.0e9
CAND_CUT = next(c for c in range(PEER_TOPK + 1) if (c + 1) ** 2 > PEER_TOPK)


def _sigmoid(x):
    return 1.0 / (1.0 + jnp.exp(-x))


def _silu(x):
    return x * _sigmoid(x)


def _gelu(x):
    return 0.5 * x * (1.0 + jnp.tanh(0.7978845608028654 * (x + 0.044715 * (x * x * x))))


def _softplus(x):
    return jnp.maximum(x, 0.0) + jnp.log(1.0 + jnp.exp(-jnp.abs(x)))


def _rms(x, w):
    return x * lax.rsqrt(jnp.mean(x * x, axis=-1, keepdims=True) + EPS) * w


def _params(sem):
    return pltpu.CompilerParams(dimension_semantics=sem, vmem_limit_bytes=VMEM_LIMIT)


def _row_tiling(B, L):
    if L >= ROW_TILE:
        assert L % ROW_TILE == 0
        return 1, ROW_TILE
    bt = min(B, ROW_TILE // L)
    assert B % bt == 0 and L % SUBLANES == 0
    return bt, L


def _mod_kernel(c_ref, w_ref, b_ref, o_ref):
    c = c_ref[...]
    o_ref[0] = jnp.dot(_silu(c), w_ref[0], preferred_element_type=f32, precision=HI) + b_ref[0]


def _mod(c, w_mod, b_mod):
    B = c.shape[0]
    n = w_mod.shape[-1]
    tn = 1536
    return pl.pallas_call(
        _mod_kernel,
        grid=(DEPTH, n // tn),
        in_specs=[pl.BlockSpec((B, D_MODEL), lambda l, j: (0, 0)),
                  pl.BlockSpec((1, D_MODEL, tn), lambda l, j: (l, 0, j)),
                  pl.BlockSpec((1, 1, tn), lambda l, j: (l, 0, j))],
        out_specs=pl.BlockSpec((1, B, tn), lambda l, j: (l, 0, j)),
        out_shape=jax.ShapeDtypeStruct((DEPTH, B, n), f32),
        compiler_params=_params(("arbitrary", "arbitrary")),
        name="mod",
    )(c, w_mod, b_mod.reshape(DEPTH, 1, n))


def _inproj_kernel(x_ref, sc_ref, sh_ref, lnw_ref, *refs):
    n = len(refs) // 2
    w_refs, o_refs = refs[:n], refs[n:]
    bt, lt, _ = x_ref.shape
    h = _rms(x_ref[...], lnw_ref[...]) * (1.0 + sc_ref[...]) + sh_ref[...]
    hb = h.reshape(bt * lt, D_MODEL).astype(bf16)
    for w_ref, o_ref in zip(w_refs, o_refs):
        o_ref[...] = jnp.dot(hb, w_ref[...], preferred_element_type=f32).reshape(o_ref.shape)


def _inproj(x, sc, sh, lnw, ws):
    B, L, _ = x.shape
    bt, lt = _row_tiling(B, L)
    row = lambda i, j: (i, j, 0)
    const = lambda i, j: (0, 0)
    return pl.pallas_call(
        _inproj_kernel,
        grid=(B // bt, L // lt),
        in_specs=[pl.BlockSpec((bt, lt, D_MODEL), row),
                  pl.BlockSpec((bt, 1, D_MODEL), lambda i, j: (i, 0, 0)),
                  pl.BlockSpec((bt, 1, D_MODEL), lambda i, j: (i, 0, 0)),
                  pl.BlockSpec((1, D_MODEL), const)]
                 + [pl.BlockSpec(w.shape, const) for w in ws],
        out_specs=[pl.BlockSpec((bt, lt, w.shape[1]), row) for w in ws],
        out_shape=[jax.ShapeDtypeStruct((B, L, w.shape[1]), f32) for w in ws],
        compiler_params=_params(("arbitrary", "arbitrary")),
        name="inproj",
    )(x, sc, sh, lnw, *ws)


def _conv_step(x_ref, buf0_ref, w_ref, b_ref, xpad, first):
    lv = x_ref.shape[1]

    @pl.when(first)
    def _():
        xpad[5:8, :] = buf0_ref[0]

    xpad[8:8 + lv, :] = x_ref[0]
    acc = b_ref[...] + w_ref[0:1, :] * xpad[5:5 + lv, :]
    for k in range(1, CONV_W):
        acc = acc + w_ref[k:k + 1, :] * xpad[5 + k:5 + k + lv, :]
    hist = xpad[5 + lv:8 + lv, :]
    xpad[5:8, :] = hist
    return acc, hist


def _pad_rows(x, rows):
    if x.shape[0] == rows:
        return x
    return jnp.concatenate([x, jnp.zeros((rows - x.shape[0], x.shape[1]), x.dtype)], axis=0)


def _ssd_kernel(xbc_ref, dt_ref, st0_ref, buf0_ref, cw_ref, cb_ref, dtb_ref, alog_ref, dvec_ref,
                expand_ref, gmask_ref, y_ref, st_ref, buf_ref, xpad, st):
    c = pl.program_id(1)
    last = c == pl.num_programs(1) - 1
    lv = xbc_ref.shape[1]
    Q = SSD_CHUNK

    @pl.when(c == 0)
    def _():
        s0 = st0_ref[0]
        st[...] = jnp.concatenate([s0, s0], axis=0) * gmask_ref[...]

    acc, hist = _conv_step(xbc_ref, buf0_ref, cw_ref, cb_ref, xpad, c == 0)
    xbc = _pad_rows(_silu(acc), Q)
    xs = xbc[:, :SSD_INNER]
    bm = xbc[:, SSD_INNER:SSD_INNER + SSD_BC]
    cm = xbc[:, SSD_INNER + SSD_BC:]

    lane = lax.broadcasted_iota(jnp.int32, (Q, LANES), 1)
    lane_v = lax.broadcasted_iota(jnp.int32, (lv, LANES), 1)
    dtv = _pad_rows(jnp.where(lane_v < SSD_HEADS, _softplus(dt_ref[0] + dtb_ref[...]), 0.0), Q)
    da = dtv * (-jnp.exp(alog_ref[...]))
    row = lax.broadcasted_iota(jnp.int32, (Q, Q), 0)
    col = lax.broadcasted_iota(jnp.int32, (Q, Q), 1)
    causal = row >= col
    acum = jnp.dot(causal.astype(f32), da, preferred_element_type=f32, precision=HI)
    acum_t = acum.T
    dt_t = dtv.T
    bm_t = bm.T
    alast = acum[Q - 1:Q, :]
    expand = expand_ref[...]
    ex = lambda v: jnp.dot(v, expand, preferred_element_type=f32, precision=HI)
    w_end = ex(jnp.exp(alast - acum) * dtv)
    e_in = ex(jnp.exp(acum))
    small = jnp.concatenate([jnp.exp(alast), dvec_ref[...], jnp.zeros((SUBLANES - 2, LANES), f32)], axis=0)
    small = ex(small)
    c_dec, d_full = small[0:1, :], small[1:2, :]

    st_in = st[...]
    y_off = jnp.dot(cm.astype(bf16), st_in.astype(bf16), preferred_element_type=f32) * e_in
    st_new = jnp.dot(bm_t.astype(bf16), (xs * w_end).astype(bf16), preferred_element_type=f32)
    st[...] = c_dec * st_in + st_new * gmask_ref[...]

    cb16, bm16 = cm.astype(bf16), bm.astype(bf16)
    nt = (((1,), (1,)), ((), ()))
    cbs = [lax.dot_general(jnp.where((lane >= g * SSD_STATE) & (lane < (g + 1) * SSD_STATE), cb16, 0), bm16, nt,
                           preferred_element_type=f32) for g in range(SSD_GROUPS)]
    hpg = SSD_HEADS // SSD_GROUPS
    ys = []
    for j in range(SSD_HEADS // 2):
        xp = xs[:, j * LANES:(j + 1) * LANES]
        wts, xsel = [], []
        for s in range(2):
            h = 2 * j + s
            seg = acum[:, h:h + 1] - acum_t[h:h + 1, :]
            dec = jnp.exp(jnp.where(causal, seg, NEG))
            wts.append((cbs[h // hpg] * dec * dt_t[h:h + 1, :]).astype(bf16))
            half = (lane >= s * SSD_HEAD_DIM) & (lane < (s + 1) * SSD_HEAD_DIM)
            xsel.append(jnp.where(half, xp, 0.0).astype(bf16))
        ys.append(jnp.dot(jnp.concatenate(wts, axis=1), jnp.concatenate(xsel, axis=0),
                          preferred_element_type=f32))
    y = jnp.concatenate(ys, axis=1) + y_off + d_full * xs
    y_ref[0] = y[:lv]

    @pl.when(last)
    def _():
        st_ref[0] = st[0:SSD_STATE, :] + st[SSD_STATE:, :]
        buf_ref[0] = hist


def _ssd(xbc, dt, st0, buf0, cw, cb, dtb, alog, dvec, expand, gmask):
    B, L, _ = xbc.shape
    lv = min(L, SSD_CHUNK)
    assert L % lv == 0
    row = lambda b, c: (b, c, 0)
    bat = lambda b, c: (b, 0, 0)
    const = lambda b, c: (0, 0)
    return pl.pallas_call(
        _ssd_kernel,
        grid=(B, L // lv),
        in_specs=[pl.BlockSpec((1, lv, SSD_CONV_DIM), row),
                  pl.BlockSpec((1, lv, LANES), row),
                  pl.BlockSpec((1, SSD_STATE, SSD_INNER), bat),
                  pl.BlockSpec((1, CONV_W - 1, SSD_CONV_DIM), bat),
                  pl.BlockSpec((CONV_W, SSD_CONV_DIM), const),
                  pl.BlockSpec((1, SSD_CONV_DIM), const),
                  pl.BlockSpec((1, LANES), const),
                  pl.BlockSpec((1, LANES), const),
                  pl.BlockSpec((1, LANES), const),
                  pl.BlockSpec((LANES, SSD_INNER), const),
                  pl.BlockSpec((LANES, SSD_INNER), const)],
        out_specs=[pl.BlockSpec((1, lv, SSD_INNER), row),
                   pl.BlockSpec((1, SSD_STATE, SSD_INNER), bat),
                   pl.BlockSpec((1, CONV_W - 1, SSD_CONV_DIM), bat)],
        out_shape=[jax.ShapeDtypeStruct((B, L, SSD_INNER), f32),
                   jax.ShapeDtypeStruct((B, SSD_STATE, SSD_INNER), f32),
                   jax.ShapeDtypeStruct((B, CONV_W - 1, SSD_CONV_DIM), f32)],
        scratch_shapes=[pltpu.VMEM((SUBLANES + lv, SSD_CONV_DIM), f32),
                        pltpu.VMEM((LANES, SSD_INNER), f32)],
        compiler_params=_params(("arbitrary", "arbitrary")),
        name="ssd",
    )(xbc, dt, st0, buf0, cw, cb, dtb, alog, dvec, expand, gmask)


def _lru_kernel(lx_ref, h0_ref, buf0_ref, cw_ref, cb_ref, wa_ref, ba_ref, wx_ref, bx_ref, lam_ref,
                hs_ref, hout_ref, buf_ref, xpad, a_s, u_s, hcar):
    c = pl.program_id(1)
    last = c == pl.num_programs(1) - 1
    lv = lx_ref.shape[1]

    @pl.when(c == 0)
    def _():
        hcar[0:1, :] = h0_ref[0]

    xb, hist = _conv_step(lx_ref, buf0_ref, cw_ref, cb_ref, xpad, c == 0)
    x16 = xb.astype(bf16)
    r = _sigmoid(jnp.dot(x16, wa_ref[...], preferred_element_type=f32) + ba_ref[...])
    i = _sigmoid(jnp.dot(x16, wx_ref[...], preferred_element_type=f32) + bx_ref[...])
    log_a = -LRU_C * r * _softplus(-lam_ref[...])
    a_s[...] = jnp.exp(log_a)
    u_s[...] = jnp.sqrt(1.0 - jnp.exp(2.0 * log_a)) * (i * xb)

    def body(j, h):
        r0 = pl.multiple_of(j * SUBLANES, SUBLANES)
        a8 = a_s[pl.ds(r0, SUBLANES), :]
        u8 = u_s[pl.ds(r0, SUBLANES), :]
        rows = []
        for k in range(SUBLANES):
            h = a8[k:k + 1, :] * h + u8[k:k + 1, :]
            rows.append(h)
        hs_ref[0, pl.ds(r0, SUBLANES), :] = jnp.concatenate(rows, axis=0)
        return h

    h = lax.fori_loop(0, lv // SUBLANES, body, hcar[0:1, :])
    hcar[0:1, :] = h

    @pl.when(last)
    def _():
        hout_ref[0] = h
        buf_ref[0] = hist


def _lru(lx, h0, buf0, cw, cb, wa, ba, wx, bx, lam):
    B, L, _ = lx.shape
    lv = min(L, SSD_CHUNK)
    assert L % lv == 0 and lv % SUBLANES == 0
    row = lambda b, c: (b, c, 0)
    bat = lambda b, c: (b, 0, 0)
    const = lambda b, c: (0, 0)
    W = LRU_WIDTH
    return pl.pallas_call(
        _lru_kernel,
        grid=(B, L // lv),
        in_specs=[pl.BlockSpec((1, lv, W), row),
                  pl.BlockSpec((1, 1, W), bat),
                  pl.BlockSpec((1, CONV_W - 1, W), bat),
                  pl.BlockSpec((CONV_W, W), const),
                  pl.BlockSpec((1, W), const),
                  pl.BlockSpec((W, W), const),
                  pl.BlockSpec((1, W), const),
                  pl.BlockSpec((W, W), const),
                  pl.BlockSpec((1, W), const),
                  pl.BlockSpec((1, W), const)],
        out_specs=[pl.BlockSpec((1, lv, W), row),
                   pl.BlockSpec((1, 1, W), bat),
                   pl.BlockSpec((1, CONV_W - 1, W), bat)],
        out_shape=[jax.ShapeDtypeStruct((B, L, W), f32),
                   jax.ShapeDtypeStruct((B, 1, W), f32),
                   jax.ShapeDtypeStruct((B, CONV_W - 1, W), f32)],
        scratch_shapes=[pltpu.VMEM((SUBLANES + lv, W), f32),
                        pltpu.VMEM((lv, W), f32),
                        pltpu.VMEM((lv, W), f32),
                        pltpu.VMEM((SUBLANES, W), f32)],
        compiler_params=_params(("arbitrary", "arbitrary")),
        name="lru",
    )(lx, h0, buf0, cw, cb, wa, ba, wx, bx, lam)


def _merge_kernel(y_ref, z_ref, hs_ref, ly_ref, gs_ref, gl_ref, x_ref, g1_ref, sc2_ref, sh2_ref,
                  nw_ref, ln2_ref, wbs_ref, wbl_ref, wo_ref, wq_ref, x1_ref, h2_ref, q_ref):
    bt, lt, _ = x_ref.shape
    rows = bt * lt
    flat = lambda v: v.reshape(rows, v.shape[-1])
    ys = _rms(y_ref[...] * _silu(z_ref[...]), nw_ref[...])
    yl = hs_ref[...] * _gelu(ly_ref[...])
    ps = jnp.dot(flat(ys).astype(bf16), wbs_ref[...], preferred_element_type=f32)
    plr = jnp.dot(flat(yl).astype(bf16), wbl_ref[...], preferred_element_type=f32)
    merged = _sigmoid(flat(gs_ref[...])) * ps + _sigmoid(flat(gl_ref[...])) * plr
    tm = jnp.dot(merged.astype(bf16), wo_ref[...], preferred_element_type=f32)
    x1 = x_ref[...] + g1_ref[...] * tm.reshape(bt, lt, D_MODEL)
    x1_ref[...] = x1
    h2 = _rms(x1, ln2_ref[...]) * (1.0 + sc2_ref[...]) + sh2_ref[...]
    h2_ref[...] = h2
    q = jnp.dot(flat(h2).astype(bf16), wq_ref[...], preferred_element_type=f32)
    for h in range(PEER_HEADS):
        q_ref[h] = q[:, h * PEER_KEY_DIM:(h + 1) * PEER_KEY_DIM].reshape(bt, lt, PEER_KEY_DIM)


def _merge(y, z, hs, ly, gs, gl, x, g1, sc2, sh2, nw, ln2, wbs, wbl, wo, wq):
    B, L, _ = x.shape
    bt, lt = _row_tiling(B, L)
    row = lambda i, j: (i, j, 0)
    bat = lambda i, j: (i, 0, 0)
    const = lambda i, j: (0, 0)
    act = pl.BlockSpec((bt, lt, D_MODEL), row)
    vec = pl.BlockSpec((bt, 1, D_MODEL), bat)
    return pl.pallas_call(
        _merge_kernel,
        grid=(B // bt, L // lt),
        in_specs=[act] * 7 + [vec] * 3
                 + [pl.BlockSpec((1, D_MODEL), const)] * 2
                 + [pl.BlockSpec((D_MODEL, D_MODEL), const)] * 3
                 + [pl.BlockSpec((D_MODEL, PEER_HEADS * PEER_KEY_DIM), const)],
        out_specs=[act, act,
                   pl.BlockSpec((PEER_HEADS, bt, lt, PEER_KEY_DIM), lambda i, j: (0, i, j, 0))],
        out_shape=[jax.ShapeDtypeStruct((B, L, D_MODEL), f32),
                   jax.ShapeDtypeStruct((B, L, D_MODEL), f32),
                   jax.ShapeDtypeStruct((PEER_HEADS, B, L, PEER_KEY_DIM), f32)],
        compiler_params=_params(("arbitrary", "arbitrary")),
        name="merge",
    )(y, z, hs, ly, gs, gl, x, g1, sc2, sh2, nw, ln2, wbs, wbl, wo, wq)


def _top16(s, idx):
    vals, ids = [], []
    for _ in range(PEER_TOPK):
        m = jnp.max(s, axis=0, keepdims=True)
        sel = jnp.min(jnp.where(s == m, idx, BIG), axis=0, keepdims=True)
        s = jnp.where(idx == sel, NEG, s)
        vals.append(m)
        ids.append(sel)
    return jnp.concatenate(vals, axis=0), jnp.concatenate(ids, axis=0)


def _topk_kernel(q_ref, k1_ref, k2_ref, e_ref, g_ref):
    tt = q_ref.shape[1]
    half = PEER_KEY_DIM // 2
    q = q_ref[0]
    nt = (((1,), (1,)), ((), ()))
    s1 = lax.dot_general(k1_ref[...], q[:, :half], nt, preferred_element_type=f32, precision=HI)
    s2 = lax.dot_general(k2_ref[...], q[:, half:], nt, preferred_element_type=f32, precision=HI)
    key = lax.broadcasted_iota(jnp.int32, (PEER_KEYS, tt), 0).astype(f32)
    v1, i1 = _top16(s1, key)
    v2, i2 = _top16(s2, key)
    rank = lax.broadcasted_iota(jnp.int32, (PEER_TOPK, tt), 0).astype(f32)
    tail = rank >= float(CAND_CUT)
    cand = jnp.concatenate([v1[a:a + 1, :] + v2 for a in range(CAND_CUT)]
                           + [jnp.where(tail, v1 + v2[b:b + 1, :], NEG) for b in range(CAND_CUT)], axis=0)
    eidx = jnp.concatenate([i1[a:a + 1, :] * float(PEER_KEYS) + i2 for a in range(CAND_CUT)]
                           + [i1 * float(PEER_KEYS) + i2[b:b + 1, :] for b in range(CAND_CUT)], axis=0)
    flat = jnp.concatenate([rank + float(a * PEER_TOPK) for a in range(CAND_CUT)]
                           + [rank * float(PEER_TOPK) + float(b) for b in range(CAND_CUT)], axis=0)
    svs, es = [], []
    s = cand
    for _ in range(PEER_TOPK):
        m = jnp.max(s, axis=0, keepdims=True)
        sel = jnp.min(jnp.where(s == m, flat, BIG), axis=0, keepdims=True)
        hit = flat == sel
        es.append(jnp.max(jnp.where(hit, eidx, -1.0), axis=0, keepdims=True))
        s = jnp.where(hit, NEG, s)
        svs.append(m)
    sv = jnp.concatenate(svs, axis=0)
    p = jnp.exp(sv - sv[0:1, :])
    g_ref[...] = p / jnp.sum(p, axis=0, keepdims=True)
    e_ref[...] = jnp.concatenate(es, axis=0).astype(jnp.int32)


def _topk(qh, k1, k2):
    H, T, _ = qh.shape
    tt = TOPK_TILE if T % TOPK_TILE == 0 else LANES
    assert T % tt == 0
    return pl.pallas_call(
        _topk_kernel,
        grid=(T // tt, H),
        in_specs=[pl.BlockSpec((1, tt, PEER_KEY_DIM), lambda i, h: (h, i, 0)),
                  pl.BlockSpec((PEER_KEYS, PEER_KEY_DIM // 2), lambda i, h: (0, 0)),
                  pl.BlockSpec((PEER_KEYS, PEER_KEY_DIM // 2), lambda i, h: (0, 0))],
        out_specs=[pl.BlockSpec((PEER_TOPK, tt), lambda i, h: (h, i)),
                   pl.BlockSpec((PEER_TOPK, tt), lambda i, h: (h, i))],
        out_shape=[jax.ShapeDtypeStruct((PEER_SEL, T), jnp.int32),
                   jax.ShapeDtypeStruct((PEER_SEL, T), f32)],
        compiler_params=_params(("arbitrary", "arbitrary")),
        name="topk",
    )(qh, k1, k2)


def _peer_kernel(h_ref, x1_ref, gt_ref, g2_ref, fw_ref, e_hbm, uv_hbm, o_ref, *scratch, rows_per_batch, final):
    bufs = scratch[:PEER_BUFS]
    esm, sem_uv, sem_e = scratch[PEER_BUFS:]
    tm = h_ref.shape[0]
    i = pl.program_id(0)
    ecp = pltpu.make_async_copy(e_hbm.at[pl.ds(i * (tm * PEER_SEL), tm * PEER_SEL)], esm, sem_e)
    ecp.start()
    ecp.wait()

    def push(t, slot, ks):
        for k in ks:
            r0 = pl.multiple_of(esm[t * PEER_SEL + k] * UV_ROWS, UV_ROWS)
            pltpu.make_async_copy(uv_hbm.at[pl.ds(r0, UV_ROWS), :], bufs[slot].at[pl.ds(k * UV_ROWS, UV_ROWS), :],
                                  sem_uv.at[slot]).start(priority=k % 2)

    def wait(slot):
        pltpu.make_async_copy(uv_hbm.at[pl.ds(0, PEER_SEL * UV_ROWS), :], bufs[slot], sem_uv.at[slot]).wait()

    lane = lax.broadcasted_iota(jnp.int32, (PEER_SEL, tm), 1)
    nch = D_MODEL // LANES
    per = PEER_SEL // (2 * nch)

    def token(t, slot, nxt, nslot):
        buf = bufs[slot]
        hrow = h_ref[pl.ds(t, 1), :]
        words = lambda c: buf[pl.ds(c, PEER_SEL, stride=UV_ROWS), :]
        u_of = lambda w: lax.bitcast_convert_type(w & jnp.uint32(0xFFFF0000), f32)
        v_of = lambda w: lax.bitcast_convert_type(w << 16, f32)
        acc = None
        for c in range(nch):
            if nxt is not None:
                push(nxt, nslot, range(c * per, (c + 1) * per))
            p = u_of(words(c)) * hrow[:, c * LANES:(c + 1) * LANES]
            acc = p if acc is None else acc + p
        s = jnp.sum(acc, axis=1, keepdims=True)
        gcol = jnp.sum(jnp.where(lane == t, gt_ref[...], 0.0), axis=1, keepdims=True)
        a = _gelu(s) * gcol
        outs = []
        for c in range(nch):
            if nxt is not None:
                push(nxt, nslot, range((nch + c) * per, (nch + c + 1) * per))
            outs.append(jnp.sum(v_of(words(c)) * a, axis=0, keepdims=True))
        o = jnp.concatenate(outs, axis=1)
        g2 = g2_ref[t // rows_per_batch] if rows_per_batch < tm else g2_ref[0]
        o_ref[pl.ds(t, 1), :] = x1_ref[pl.ds(t, 1), :] + g2 * o

    ahead = PEER_BUFS - 1
    for s in range(ahead):
        push(s, s, range(PEER_SEL))

    def group(j, carry):
        t0 = PEER_BUFS * j
        for s in range(PEER_BUFS):
            wait(s)
            token(t0 + s, s, t0 + s + ahead, (s + ahead) % PEER_BUFS)
        return carry

    lax.fori_loop(0, tm // PEER_BUFS - 1, group, 0)
    t0 = tm - PEER_BUFS
    for s in range(PEER_BUFS):
        wait(s)
        token(t0 + s, s, t0 + s + ahead if s == 0 else None, (s + ahead) % PEER_BUFS)
    if final:
        o_ref[...] = _rms(o_ref[...], fw_ref[...])


def _peer(h2, x1, gt, g2, fw, e_flat, uv, rows_per_batch, final):
    T = h2.shape[0]
    tm = LANES
    assert T % tm == 0
    if rows_per_batch >= tm:
        assert rows_per_batch % tm == 0
        g2_spec = pl.BlockSpec((1, 1, D_MODEL), lambda i: (i * tm // rows_per_batch, 0, 0))
    else:
        assert tm % rows_per_batch == 0
        nb = tm // rows_per_batch
        g2_spec = pl.BlockSpec((nb, 1, D_MODEL), lambda i: (i, 0, 0))
    tok = pl.BlockSpec((tm, D_MODEL), lambda i: (i, 0))
    return pl.pallas_call(
        functools.partial(_peer_kernel, rows_per_batch=rows_per_batch, final=final),
        grid=(T // tm,),
        in_specs=[tok, tok,
                  pl.BlockSpec((PEER_SEL, tm), lambda i: (0, i)),
                  g2_spec,
                  pl.BlockSpec((1, D_MODEL), lambda i: (0, 0)),
                  pl.BlockSpec(memory_space=pl.ANY),
                  pl.BlockSpec(memory_space=pl.ANY)],
        out_specs=tok,
        out_shape=jax.ShapeDtypeStruct((T, D_MODEL), f32),
        scratch_shapes=[pltpu.VMEM((PEER_SEL * UV_ROWS, LANES), jnp.uint32)] * PEER_BUFS
                       + [pltpu.SMEM((tm * PEER_SEL,), jnp.int32),
                          pltpu.SemaphoreType.DMA((PEER_BUFS,)),
                          pltpu.SemaphoreType.DMA(())],
        compiler_params=_params(("arbitrary",)),
        name="peer",
    )(h2, x1, gt, g2, fw, e_flat, uv)


def _block_diag(w):
    n, d, _ = w.shape
    eye = jnp.eye(n, dtype=w.dtype)
    return (eye[:, None, :, None] * w[:, :, None, :]).reshape(n * d, n * d)


def _pad_lanes(v):
    return jnp.pad(v, (0, LANES - v.shape[0])).reshape(1, LANES)


def _prep_layer(l, p):
    w_in = p["w_in"][l]
    ws = [w_in[:, :Z_END], w_in[:, Z_END:XBC_END],
          jnp.pad(w_in[:, XBC_END:DT_END], ((0, 0), (0, LANES - SSD_HEADS))),
          w_in[:, DT_END:LX_END], w_in[:, LX_END:LY_END], w_in[:, LY_END:GS_END], w_in[:, GS_END:]]
    head = jnp.arange(SSD_INNER) // SSD_HEAD_DIM
    expand = (jnp.arange(LANES)[:, None] == head[None, :]).astype(f32)
    gmask = ((jnp.arange(LANES)[:, None] // SSD_STATE) == (head[None, :] // (SSD_HEADS // SSD_GROUPS))).astype(f32)
    bits = lambda t: lax.bitcast_convert_type(t.astype(bf16), jnp.uint16).astype(jnp.uint32)
    uv = (bits(p["peer_u"][l]) << 16) | bits(p["peer_v"][l])
    return dict(
        ws=[w.astype(bf16) for w in ws],
        ln1=p["ln1_w"][l].reshape(1, -1), ln2=p["ln2_w"][l].reshape(1, -1),
        ssd_cw=p["ssd_conv_w"][l], ssd_cb=p["ssd_conv_b"][l].reshape(1, -1),
        dtb=_pad_lanes(p["ssd_dt_bias"][l]), alog=_pad_lanes(p["ssd_a_log"][l]), dvec=_pad_lanes(p["ssd_d"][l]),
        expand=expand, gmask=gmask, nw=p["ssd_norm_w"][l].reshape(1, -1),
        lru_cw=p["lru_conv_w"][l], lru_cb=p["lru_conv_b"][l].reshape(1, -1),
        wa=_block_diag(p["lru_wa"][l]).astype(bf16), ba=p["lru_ba"][l].reshape(1, -1),
        wx=_block_diag(p["lru_wx"][l]).astype(bf16), bx=p["lru_bx"][l].reshape(1, -1),
        lam=p["lru_lambda"][l].reshape(1, -1),
        wbs=p["w_br_ssd"][l].astype(bf16), wbl=p["w_br_lru"][l].astype(bf16), wo=p["w_out"][l].astype(bf16),
        wq=p["peer_wq"][l].astype(bf16), k1=p["peer_k1"][l], k2=p["peer_k2"][l],
        uv=uv.reshape(PEER_EXPERTS * UV_ROWS, LANES),
    )


def _state_in(s):
    return s.transpose(0, 3, 1, 2).reshape(s.shape[0], SSD_STATE, SSD_INNER)


def _state_out(s):
    return s.reshape(s.shape[0], SSD_STATE, SSD_HEADS, SSD_HEAD_DIM).transpose(0, 2, 3, 1)


def _layer(x, mod, ssd_h0, ssd_buf, lru_h0, lru_buf, w, fw, final):
    B, L, _ = x.shape
    sh1, sc1, g1, sh2, sc2, g2 = [m.reshape(B, 1, D_MODEL) for m in jnp.split(mod, 6, axis=-1)]
    z, xbc, dt, lx, ly, gs, gl = _inproj(x, sc1, sh1, w["ln1"], w["ws"])
    y, st, sbuf = _ssd(xbc, dt, _state_in(ssd_h0), ssd_buf, w["ssd_cw"], w["ssd_cb"], w["dtb"], w["alog"],
                       w["dvec"], w["expand"], w["gmask"])
    hs, lh, lbuf = _lru(lx, lru_h0.reshape(B, 1, LRU_WIDTH), lru_buf, w["lru_cw"], w["lru_cb"], w["wa"], w["ba"],
                        w["wx"], w["bx"], w["lam"])
    x1, h2, qh = _merge(y, z, hs, ly, gs, gl, x, g1, sc2, sh2, w["nw"], w["ln2"], w["wbs"], w["wbl"], w["wo"],
                        w["wq"])
    T = B * L
    e_t, g_t = _topk(qh.reshape(PEER_HEADS, T, PEER_KEY_DIM), w["k1"], w["k2"])
    x2 = _peer(h2.reshape(T, D_MODEL), x1.reshape(T, D_MODEL), g_t, g2, fw, e_t.T.reshape(T * PEER_SEL), w["uv"],
               L, final)
    return x2.reshape(B, L, D_MODEL), _state_out(st), sbuf, lh.reshape(B, LRU_WIDTH), lbuf


def _trunk(x, c, ssd_h, ssd_buf, lru_h, lru_buf, p, layers):
    mod = _mod(c, p["w_mod"], p["b_mod"])
    fw = p["final_norm_w"].reshape(1, -1)
    outs = [[], [], [], []]
    for l in range(DEPTH):
        x, *new = _layer(x, mod[l], ssd_h[l], ssd_buf[l], lru_h[l], lru_buf[l], layers[l], fw, l == DEPTH - 1)
        for acc, v in zip(outs, new):
            acc.append(v)
    return (x,) + tuple(jnp.stack(o) for o in outs)


def kernel(x_prompt, x_sample, c_prompt, c_sample, state_ssd, state_ssd_conv, state_lru, state_lru_conv, ln1_w, ln2_w, w_mod, b_mod, w_in, ssd_conv_w, ssd_conv_b, ssd_dt_bias, ssd_a_log, ssd_d, ssd_norm_w, lru_conv_w, lru_conv_b, lru_wa, lru_ba, lru_wx, lru_bx, lru_lambda, w_br_ssd, w_br_lru, w_out, peer_wq, peer_k1, peer_k2, peer_u, peer_v, final_norm_w):
    p = dict(ln1_w=ln1_w, ln2_w=ln2_w, w_mod=w_mod, b_mod=b_mod, w_in=w_in, ssd_conv_w=ssd_conv_w,
             ssd_conv_b=ssd_conv_b, ssd_dt_bias=ssd_dt_bias, ssd_a_log=ssd_a_log, ssd_d=ssd_d,
             ssd_norm_w=ssd_norm_w, lru_conv_w=lru_conv_w, lru_conv_b=lru_conv_b, lru_wa=lru_wa, lru_ba=lru_ba,
             lru_wx=lru_wx, lru_bx=lru_bx, lru_lambda=lru_lambda, w_br_ssd=w_br_ssd, w_br_lru=w_br_lru,
             w_out=w_out, peer_wq=peer_wq, peer_k1=peer_k1, peer_k2=peer_k2, peer_u=peer_u, peer_v=peer_v,
             final_norm_w=final_norm_w)
    layers = [_prep_layer(l, p) for l in range(DEPTH)]
    bp = x_prompt.shape[0]
    zeros = lambda *s: jnp.zeros((DEPTH, bp) + s, f32)
    ys = _trunk(x_sample, c_sample, state_ssd, state_ssd_conv, state_lru, state_lru_conv, p, layers)
    yp = _trunk(x_prompt, c_prompt, zeros(SSD_HEADS, SSD_HEAD_DIM, SSD_STATE), zeros(CONV_W - 1, SSD_CONV_DIM),
                zeros(LRU_WIDTH), zeros(CONV_W - 1, LRU_WIDTH), p, layers)
    return (yp[0], ys[0]) + yp[1:] + ys[1:]
```

```python
import functools

import jax
import jax.numpy as jnp
from jax import lax
from jax.experimental import pallas as pl
from jax.experimental.pallas import tpu as pltpu

f32 = jnp.float32
bf16 = jnp.bfloat16
HI = lax.Precision.HIGHEST

D_MODEL = 1024
DEPTH = 2
CONV_W = 4
EPS = 1e-6
SSD_HEADS = 16
SSD_HEAD_DIM = 64
SSD_INNER = SSD_HEADS * SSD_HEAD_DIM
SSD_GROUPS = 2
SSD_STATE = 64
SSD_BC = SSD_GROUPS * SSD_STATE
SSD_CONV_DIM = SSD_INNER + 2 * SSD_BC
LRU_WIDTH = 1024
LRU_BLOCKS = 16
LRU_C = 8.0
PEER_HEADS = 8
PEER_KEYS = 128
PEER_EXPERTS = PEER_KEYS * PEER_KEYS
PEER_KEY_DIM = 256
PEER_TOPK = 16
PEER_SEL = PEER_HEADS * PEER_TOPK
Z_END = SSD_INNER
XBC_END = Z_END + SSD_CONV_DIM
DT_END = XBC_END + SSD_HEADS
LX_END = DT_END + LRU_WIDTH
LY_END = LX_END + LRU_WIDTH
GS_END = LY_END + D_MODEL

LANES = 128
SUBLANES = 8
SSD_CHUNK = 128
ROW_TILE = 256
UV_ROWS = D_MODEL // LANES
TOPK_SLOTS = 16
PEER_BUFS = 8
VMEM_LIMIT = 56 * 1024 * 1024
NEG = -3.0e38
BIG = 1.0e9
CAND_CUT = next(c for c in range(PEER_TOPK + 1) if (c + 1) ** 2 > PEER_TOPK)


def _sigmoid(x):
    return 1.0 / (1.0 + jnp.exp(-x))


def _silu(x):
    return x * _sigmoid(x)


def _gelu(x):
    return 0.5 * x * (1.0 + jnp.tanh(0.7978845608028654 * (x + 0.044715 * (x * x * x))))


def _softplus(x):
    return jnp.maximum(x, 0.0) + jnp.log(1.0 + jnp.exp(-jnp.abs(x)))


def _rms(x, w):
    return x * lax.rsqrt(jnp.mean(x * x, axis=-1, keepdims=True) + EPS) * w


def _params(sem):
    return pltpu.CompilerParams(dimension_semantics=sem, vmem_limit_bytes=VMEM_LIMIT)


def _row_tiling(B, L):
    if L >= ROW_TILE:
        assert L % ROW_TILE == 0
        return 1, ROW_TILE
    bt = min(B, ROW_TILE // L)
    assert B % bt == 0 and L % SUBLANES == 0
    return bt, L


def _mod_kernel(c_ref, w_ref, b_ref, o_ref):
    c = c_ref[...]
    o_ref[0] = jnp.dot(_silu(c), w_ref[0], preferred_element_type=f32, precision=HI) + b_ref[0]


def _mod(c, w_mod, b_mod):
    B = c.shape[0]
    n = w_mod.shape[-1]
    tn = 1536
    return pl.pallas_call(
        _mod_kernel,
        grid=(DEPTH, n // tn),
        in_specs=[pl.BlockSpec((B, D_MODEL), lambda l, j: (0, 0)),
                  pl.BlockSpec((1, D_MODEL, tn), lambda l, j: (l, 0, j)),
                  pl.BlockSpec((1, 1, tn), lambda l, j: (l, 0, j))],
        out_specs=pl.BlockSpec((1, B, tn), lambda l, j: (l, 0, j)),
        out_shape=jax.ShapeDtypeStruct((DEPTH, B, n), f32),
        compiler_params=_params(("arbitrary", "arbitrary")),
        name="mod",
    )(c, w_mod, b_mod.reshape(DEPTH, 1, n))


def _inproj_kernel(x_ref, sc_ref, sh_ref, lnw_ref, *refs):
    n = len(refs) // 2
    w_refs, o_refs = refs[:n], refs[n:]
    bt, lt, _ = x_ref.shape
    h = _rms(x_ref[...], lnw_ref[...]) * (1.0 + sc_ref[...]) + sh_ref[...]
    hb = h.reshape(bt * lt, D_MODEL).astype(bf16)
    for w_ref, o_ref in zip(w_refs, o_refs):
        o_ref[...] = jnp.dot(hb, w_ref[...], preferred_element_type=f32).reshape(o_ref.shape)


def _inproj(x, sc, sh, lnw, ws):
    B, L, _ = x.shape
    bt, lt = _row_tiling(B, L)
    row = lambda i, j: (i, j, 0)
    const = lambda i, j: (0, 0)
    return pl.pallas_call(
        _inproj_kernel,
        grid=(B // bt, L // lt),
        in_specs=[pl.BlockSpec((bt, lt, D_MODEL), row),
                  pl.BlockSpec((bt, 1, D_MODEL), lambda i, j: (i, 0, 0)),
                  pl.BlockSpec((bt, 1, D_MODEL), lambda i, j: (i, 0, 0)),
                  pl.BlockSpec((1, D_MODEL), const)]
                 + [pl.BlockSpec(w.shape, const) for w in ws],
        out_specs=[pl.BlockSpec((bt, lt, w.shape[1]), row) for w in ws],
        out_shape=[jax.ShapeDtypeStruct((B, L, w.shape[1]), f32) for w in ws],
        compiler_params=_params(("arbitrary", "arbitrary")),
        name="inproj",
    )(x, sc, sh, lnw, *ws)


def _conv_step(x_ref, buf0_ref, w_ref, b_ref, xpad, first):
    lv = x_ref.shape[1]

    @pl.when(first)
    def _():
        xpad[5:8, :] = buf0_ref[0]

    xpad[8:8 + lv, :] = x_ref[0]
    acc = b_ref[...] + w_ref[0:1, :] * xpad[5:5 + lv, :]
    for k in range(1, CONV_W):
        acc = acc + w_ref[k:k + 1, :] * xpad[5 + k:5 + k + lv, :]
    hist = xpad[5 + lv:8 + lv, :]
    xpad[5:8, :] = hist
    return acc, hist


def _pad_rows(x, rows):
    if x.shape[0] == rows:
        return x
    return jnp.concatenate([x, jnp.zeros((rows - x.shape[0], x.shape[1]), x.dtype)], axis=0)


def _ssd_kernel(xbc_ref, dt_ref, st0_ref, buf0_ref, cw_ref, cb_ref, dtb_ref, alog_ref, dvec_ref,
                expand_ref, gmask_ref, y_ref, st_ref, buf_ref, xpad, st):
    c = pl.program_id(1)
    last = c == pl.num_programs(1) - 1
    lv = xbc_ref.shape[1]
    Q = SSD_CHUNK

    @pl.when(c == 0)
    def _():
        s0 = st0_ref[0]
        st[...] = jnp.concatenate([s0, s0], axis=0) * gmask_ref[...]

    acc, hist = _conv_step(xbc_ref, buf0_ref, cw_ref, cb_ref, xpad, c == 0)
    xbc = _pad_rows(_silu(acc), Q)
    xs = xbc[:, :SSD_INNER]
    bm = xbc[:, SSD_INNER:SSD_INNER + SSD_BC]
    cm = xbc[:, SSD_INNER + SSD_BC:]

    lane = lax.broadcasted_iota(jnp.int32, (Q, LANES), 1)
    lane_v = lax.broadcasted_iota(jnp.int32, (lv, LANES), 1)
    dtv = _pad_rows(jnp.where(lane_v < SSD_HEADS, _softplus(dt_ref[0] + dtb_ref[...]), 0.0), Q)
    da = dtv * (-jnp.exp(alog_ref[...]))
    row = lax.broadcasted_iota(jnp.int32, (Q, Q), 0)
    col = lax.broadcasted_iota(jnp.int32, (Q, Q), 1)
    causal = row >= col
    acum = jnp.dot(causal.astype(f32), da, preferred_element_type=f32, precision=HI)
    acum_t = acum.T
    dt_t = dtv.T
    bm_t = bm.T
    alast = acum[Q - 1:Q, :]
    expand = expand_ref[...]
    ex = lambda v: jnp.dot(v, expand, preferred_element_type=f32, precision=HI)
    w_end = ex(jnp.exp(alast - acum) * dtv)
    e_in = ex(jnp.exp(acum))
    small = jnp.concatenate([jnp.exp(alast), dvec_ref[...], jnp.zeros((SUBLANES - 2, LANES), f32)], axis=0)
    small = ex(small)
    c_dec, d_full = small[0:1, :], small[1:2, :]

    st_in = st[...]
    y_off = jnp.dot(cm.astype(bf16), st_in.astype(bf16), preferred_element_type=f32) * e_in
    st_new = jnp.dot(bm_t.astype(bf16), (xs * w_end).astype(bf16), preferred_element_type=f32)
    st[...] = c_dec * st_in + st_new * gmask_ref[...]

    cb16, bm16 = cm.astype(bf16), bm.astype(bf16)
    nt = (((1,), (1,)), ((), ()))
    cbs = [lax.dot_general(jnp.where((lane >= g * SSD_STATE) & (lane < (g + 1) * SSD_STATE), cb16, 0), bm16, nt,
                           preferred_element_type=f32) for g in range(SSD_GROUPS)]
    hpg = SSD_HEADS // SSD_GROUPS
    ys = []
    for j in range(SSD_HEADS // 2):
        xp = xs[:, j * LANES:(j + 1) * LANES]
        wts, xsel = [], []
        for s in range(2):
            h = 2 * j + s
            seg = acum[:, h:h + 1] - acum_t[h:h + 1, :]
            dec = jnp.exp(jnp.where(causal, seg, NEG))
            wts.append((cbs[h // hpg] * dec * dt_t[h:h + 1, :]).astype(bf16))
            half = (lane >= s * SSD_HEAD_DIM) & (lane < (s + 1) * SSD_HEAD_DIM)
            xsel.append(jnp.where(half, xp, 0.0).astype(bf16))
        ys.append(jnp.dot(jnp.concatenate(wts, axis=1), jnp.concatenate(xsel, axis=0),
                          preferred_element_type=f32))
    y = jnp.concatenate(ys, axis=1) + y_off + d_full * xs
    y_ref[0] = y[:lv]

    @pl.when(last)
    def _():
        st_ref[0] = st[0:SSD_STATE, :] + st[SSD_STATE:, :]
        buf_ref[0] = hist


def _ssd(xbc, dt, st0, buf0, cw, cb, dtb, alog, dvec, expand, gmask):
    B, L, _ = xbc.shape
    lv = min(L, SSD_CHUNK)
    assert L % lv == 0
    row = lambda b, c: (b, c, 0)
    bat = lambda b, c: (b, 0, 0)
    const = lambda b, c: (0, 0)
    return pl.pallas_call(
        _ssd_kernel,
        grid=(B, L // lv),
        in_specs=[pl.BlockSpec((1, lv, SSD_CONV_DIM), row),
                  pl.BlockSpec((1, lv, LANES), row),
                  pl.BlockSpec((1, SSD_STATE, SSD_INNER), bat),
                  pl.BlockSpec((1, CONV_W - 1, SSD_CONV_DIM), bat),
                  pl.BlockSpec((CONV_W, SSD_CONV_DIM), const),
                  pl.BlockSpec((1, SSD_CONV_DIM), const),
                  pl.BlockSpec((1, LANES), const),
                  pl.BlockSpec((1, LANES), const),
                  pl.BlockSpec((1, LANES), const),
                  pl.BlockSpec((LANES, SSD_INNER), const),
                  pl.BlockSpec((LANES, SSD_INNER), const)],
        out_specs=[pl.BlockSpec((1, lv, SSD_INNER), row),
                   pl.BlockSpec((1, SSD_STATE, SSD_INNER), bat),
                   pl.BlockSpec((1, CONV_W - 1, SSD_CONV_DIM), bat)],
        out_shape=[jax.ShapeDtypeStruct((B, L, SSD_INNER), f32),
                   jax.ShapeDtypeStruct((B, SSD_STATE, SSD_INNER), f32),
                   jax.ShapeDtypeStruct((B, CONV_W - 1, SSD_CONV_DIM), f32)],
        scratch_shapes=[pltpu.VMEM((SUBLANES + lv, SSD_CONV_DIM), f32),
                        pltpu.VMEM((LANES, SSD_INNER), f32)],
        compiler_params=_params(("arbitrary", "arbitrary")),
        name="ssd",
    )(xbc, dt, st0, buf0, cw, cb, dtb, alog, dvec, expand, gmask)


def _lru_kernel(lx_ref, h0_ref, buf0_ref, cw_ref, cb_ref, wa_ref, ba_ref, wx_ref, bx_ref, lam_ref,
                hs_ref, hout_ref, buf_ref, xpad, a_s, u_s, hcar):
    c = pl.program_id(1)
    last = c == pl.num_programs(1) - 1
    lv = lx_ref.shape[1]

    @pl.when(c == 0)
    def _():
        hcar[0:1, :] = h0_ref[0]

    xb, hist = _conv_step(lx_ref, buf0_ref, cw_ref, cb_ref, xpad, c == 0)
    x16 = xb.astype(bf16)
    r = _sigmoid(jnp.dot(x16, wa_ref[...], preferred_element_type=f32) + ba_ref[...])
    i = _sigmoid(jnp.dot(x16, wx_ref[...], preferred_element_type=f32) + bx_ref[...])
    log_a = -LRU_C * r * _softplus(-lam_ref[...])
    a_s[...] = jnp.exp(log_a)
    u_s[...] = jnp.sqrt(1.0 - jnp.exp(2.0 * log_a)) * (i * xb)

    def body(j, h):
        r0 = pl.multiple_of(j * SUBLANES, SUBLANES)
        a8 = a_s[pl.ds(r0, SUBLANES), :]
        u8 = u_s[pl.ds(r0, SUBLANES), :]
        rows = []
        for k in range(SUBLANES):
            h = a8[k:k + 1, :] * h + u8[k:k + 1, :]
            rows.append(h)
        hs_ref[0, pl.ds(r0, SUBLANES), :] = jnp.concatenate(rows, axis=0)
        return h

    h = lax.fori_loop(0, lv // SUBLANES, body, hcar[0:1, :])
    hcar[0:1, :] = h

    @pl.when(last)
    def _():
        hout_ref[0] = h
        buf_ref[0] = hist


def _lru(lx, h0, buf0, cw, cb, wa, ba, wx, bx, lam):
    B, L, _ = lx.shape
    lv = min(L, SSD_CHUNK)
    assert L % lv == 0 and lv % SUBLANES == 0
    row = lambda b, c: (b, c, 0)
    bat = lambda b, c: (b, 0, 0)
    const = lambda b, c: (0, 0)
    W = LRU_WIDTH
    return pl.pallas_call(
        _lru_kernel,
        grid=(B, L // lv),
        in_specs=[pl.BlockSpec((1, lv, W), row),
                  pl.BlockSpec((1, 1, W), bat),
                  pl.BlockSpec((1, CONV_W - 1, W), bat),
                  pl.BlockSpec((CONV_W, W), const),
                  pl.BlockSpec((1, W), const),
                  pl.BlockSpec((W, W), const),
                  pl.BlockSpec((1, W), const),
                  pl.BlockSpec((W, W), const),
                  pl.BlockSpec((1, W), const),
                  pl.BlockSpec((1, W), const)],
        out_specs=[pl.BlockSpec((1, lv, W), row),
                   pl.BlockSpec((1, 1, W), bat),
                   pl.BlockSpec((1, CONV_W - 1, W), bat)],
        out_shape=[jax.ShapeDtypeStruct((B, L, W), f32),
                   jax.ShapeDtypeStruct((B, 1, W), f32),
                   jax.ShapeDtypeStruct((B, CONV_W - 1, W), f32)],
        scratch_shapes=[pltpu.VMEM((SUBLANES + lv, W), f32),
                        pltpu.VMEM((lv, W), f32),
                        pltpu.VMEM((lv, W), f32),
                        pltpu.VMEM((SUBLANES, W), f32)],
        compiler_params=_params(("arbitrary", "arbitrary")),
        name="lru",
    )(lx, h0, buf0, cw, cb, wa, ba, wx, bx, lam)


def _merge_kernel(y_ref, z_ref, hs_ref, ly_ref, gs_ref, gl_ref, x_ref, g1_ref, sc2_ref, sh2_ref,
                  nw_ref, ln2_ref, wbs_ref, wbl_ref, wo_ref, wq_ref, x1_ref, h2_ref, q_ref):
    bt, lt, _ = x_ref.shape
    rows = bt * lt
    flat = lambda v: v.reshape(rows, v.shape[-1])
    ys = _rms(y_ref[...] * _silu(z_ref[...]), nw_ref[...])
    yl = hs_ref[...] * _gelu(ly_ref[...])
    ps = jnp.dot(flat(ys).astype(bf16), wbs_ref[...], preferred_element_type=f32)
    plr = jnp.dot(flat(yl).astype(bf16), wbl_ref[...], preferred_element_type=f32)
    merged = _sigmoid(flat(gs_ref[...])) * ps + _sigmoid(flat(gl_ref[...])) * plr
    tm = jnp.dot(merged.astype(bf16), wo_ref[...], preferred_element_type=f32)
    x1 = x_ref[...] + g1_ref[...] * tm.reshape(bt, lt, D_MODEL)
    x1_ref[...] = x1
    h2 = _rms(x1, ln2_ref[...]) * (1.0 + sc2_ref[...]) + sh2_ref[...]
    h2_ref[...] = h2
    q = jnp.dot(flat(h2).astype(bf16), wq_ref[...], preferred_element_type=f32)
    for h in range(PEER_HEADS):
        q_ref[h] = q[:, h * PEER_KEY_DIM:(h + 1) * PEER_KEY_DIM].reshape(bt, lt, PEER_KEY_DIM)


def _merge(y, z, hs, ly, gs, gl, x, g1, sc2, sh2, nw, ln2, wbs, wbl, wo, wq):
    B, L, _ = x.shape
    bt, lt = _row_tiling(B, L)
    row = lambda i, j: (i, j, 0)
    bat = lambda i, j: (i, 0, 0)
    const = lambda i, j: (0, 0)
    act = pl.BlockSpec((bt, lt, D_MODEL), row)
    vec = pl.BlockSpec((bt, 1, D_MODEL), bat)
    return pl.pallas_call(
        _merge_kernel,
        grid=(B // bt, L // lt),
        in_specs=[act] * 7 + [vec] * 3
                 + [pl.BlockSpec((1, D_MODEL), const)] * 2
                 + [pl.BlockSpec((D_MODEL, D_MODEL), const)] * 3
                 + [pl.BlockSpec((D_MODEL, PEER_HEADS * PEER_KEY_DIM), const)],
        out_specs=[act, act,
                   pl.BlockSpec((PEER_HEADS, bt, lt, PEER_KEY_DIM), lambda i, j: (0, i, j, 0))],
        out_shape=[jax.ShapeDtypeStruct((B, L, D_MODEL), f32),
                   jax.ShapeDtypeStruct((B, L, D_MODEL), f32),
                   jax.ShapeDtypeStruct((PEER_HEADS, B, L, PEER_KEY_DIM), f32)],
        compiler_params=_params(("arbitrary", "arbitrary")),
        name="merge",
    )(y, z, hs, ly, gs, gl, x, g1, sc2, sh2, nw, ln2, wbs, wbl, wo, wq)


class _HeadTopK:
    def __init__(self, q, k1, k2):
        self.q, self.k1, self.k2 = q, k1, k2
        self.tt = q.shape[0]
        self.vals, self.ids = [], []

    @staticmethod
    def _round(s, idx):
        m = jnp.max(s, axis=0, keepdims=True)
        c = jnp.where(s == m, idx, BIG)
        sel = jnp.min(c, axis=0, keepdims=True)
        return m, sel, c == sel

    def _start_halves(self):
        half = PEER_KEY_DIM // 2
        nt = (((1,), (1,)), ((), ()))
        s1 = lax.dot_general(self.k1, self.q[:, :half], nt, preferred_element_type=f32, precision=HI)
        s2 = lax.dot_general(self.k2, self.q[:, half:], nt, preferred_element_type=f32, precision=HI)
        self.s = jnp.concatenate([s1, s2], axis=1)
        self.idx = lax.broadcasted_iota(jnp.int32, (PEER_KEYS, 2 * self.tt), 0).astype(f32)

    def _start_pairs(self):
        tt = self.tt
        v, i = jnp.concatenate(self.vals, axis=0), jnp.concatenate(self.ids, axis=0)
        v1, v2, i1, i2 = v[:, :tt], v[:, tt:], i[:, :tt], i[:, tt:]
        rank = lax.broadcasted_iota(jnp.int32, (PEER_TOPK, tt), 0).astype(f32)
        tail = rank >= float(CAND_CUT)
        self.s = jnp.concatenate([v1[a:a + 1, :] + v2 for a in range(CAND_CUT)]
                                 + [jnp.where(tail, v1 + v2[b:b + 1, :], NEG) for b in range(CAND_CUT)], axis=0)
        self.eidx = jnp.concatenate([i1[a:a + 1, :] * float(PEER_KEYS) + i2 for a in range(CAND_CUT)]
                                    + [i1 * float(PEER_KEYS) + i2[b:b + 1, :] for b in range(CAND_CUT)], axis=0)
        self.idx = jnp.concatenate([rank + float(a * PEER_TOPK) for a in range(CAND_CUT)]
                                   + [rank * float(PEER_TOPK) + float(b) for b in range(CAND_CUT)], axis=0)
        self.vals, self.ids = [], []

    def step(self, j):
        per = 2 * PEER_TOPK // TOPK_SLOTS
        first_pair_slot = TOPK_SLOTS // 2
        if j == 0:
            self._start_halves()
        if j == first_pair_slot:
            self._start_pairs()
        for _ in range(per):
            m, sel, hit = self._round(self.s, self.idx)
            self.vals.append(m)
            self.ids.append(sel if j < first_pair_slot else jnp.max(jnp.where(hit, self.eidx, -1.0), axis=0, keepdims=True))
            self.s = jnp.where(hit, NEG, self.s)
        if j == TOPK_SLOTS - 1:
            sv = jnp.concatenate(self.vals, axis=0)
            p = jnp.exp(sv - sv[0:1, :])
            self.g = p / jnp.sum(p, axis=0, keepdims=True)
            self.e = jnp.concatenate(self.ids, axis=0).astype(jnp.int32)


def _peer_kernel(h_ref, x1_ref, qn_ref, q0_ref, k1_ref, k2_ref, g2_ref, fw_ref, uv_hbm, o_ref, *scratch,
                 rows_per_batch, final):
    bufs = scratch[:PEER_BUFS]
    esm, e_vm, g_vm, sem_uv, sem_e = scratch[PEER_BUFS:]
    tm = h_ref.shape[0]
    i = pl.program_id(0)
    cur, nxt_par = i % 2, (i + 1) % 2
    k1, k2 = k1_ref[...], k2_ref[...]

    def publish(tk, head, par):
        r0 = pl.multiple_of(head * PEER_TOPK, PEER_TOPK)
        e_vm[pl.ds(r0, PEER_TOPK), :] = tk.e
        g_vm[par, pl.ds(r0, PEER_TOPK), :] = tk.g

    def load_indices():
        cp = pltpu.make_async_copy(e_vm, esm, sem_e)
        cp.start()
        cp.wait()

    @pl.when(i == 0)
    def _():
        def head0(h, carry):
            tk = _HeadTopK(q0_ref[h], k1, k2)
            for j in range(TOPK_SLOTS):
                tk.step(j)
            publish(tk, h, 0)
            return carry
        lax.fori_loop(0, PEER_HEADS, head0, 0)
        load_indices()

    def push(t, slot, ks):
        for k in ks:
            r0 = pl.multiple_of(esm[k, t] * UV_ROWS, UV_ROWS)
            pltpu.make_async_copy(uv_hbm.at[pl.ds(r0, UV_ROWS), :], bufs[slot].at[pl.ds(k * UV_ROWS, UV_ROWS), :],
                                  sem_uv.at[slot]).start(priority=k % 2)

    def wait(slot):
        pltpu.make_async_copy(uv_hbm.at[pl.ds(0, PEER_SEL * UV_ROWS), :], bufs[slot], sem_uv.at[slot]).wait()

    lane = lax.broadcasted_iota(jnp.int32, (PEER_SEL, tm), 1)
    nch = D_MODEL // LANES
    per = PEER_SEL // (2 * nch)

    def token(t, slot, nxt, nslot, between):
        buf = bufs[slot]
        hrow = h_ref[pl.ds(t, 1), :]
        words = lambda c: buf[pl.ds(c, PEER_SEL, stride=UV_ROWS), :]
        u_of = lambda w: lax.bitcast_convert_type(w & jnp.uint32(0xFFFF0000), f32)
        v_of = lambda w: lax.bitcast_convert_type(w << 16, f32)
        acc = None
        for c in range(nch):
            if nxt is not None:
                push(nxt, nslot, range(c * per, (c + 1) * per))
            p = u_of(words(c)) * hrow[:, c * LANES:(c + 1) * LANES]
            acc = p if acc is None else acc + p
        s = jnp.sum(acc, axis=1, keepdims=True)
        gcol = jnp.sum(jnp.where(lane == t, g_vm[cur], 0.0), axis=1, keepdims=True)
        between()
        a = _gelu(s) * gcol
        outs = []
        for c in range(nch):
            if nxt is not None:
                push(nxt, nslot, range((nch + c) * per, (nch + c + 1) * per))
            outs.append(jnp.sum(v_of(words(c)) * a, axis=0, keepdims=True))
        o = jnp.concatenate(outs, axis=1)
        g2 = g2_ref[t // rows_per_batch] if rows_per_batch < tm else g2_ref[0]
        o_ref[pl.ds(t, 1), :] = x1_ref[pl.ds(t, 1), :] + g2 * o

    ahead = PEER_BUFS - 1
    for s in range(ahead):
        push(s, s, range(PEER_SEL))

    def head_span(head, static_t0):
        t0 = head * TOPK_SLOTS
        tk = _HeadTopK(qn_ref[head], k1, k2)
        for j in range(TOPK_SLOTS):
            slot = j % PEER_BUFS
            more = static_t0 is None or static_t0 + j + ahead < tm
            wait(slot)
            token(t0 + j, slot, t0 + j + ahead if more else None, (slot + ahead) % PEER_BUFS,
                  functools.partial(tk.step, j))
        publish(tk, head, nxt_par)

    def spans(h, carry):
        head_span(h, None)
        return carry

    n_spans = tm // TOPK_SLOTS
    lax.fori_loop(0, n_spans - 1, spans, 0)
    head_span(n_spans - 1, tm - TOPK_SLOTS)
    if final:
        o_ref[...] = _rms(o_ref[...], fw_ref[...])
    load_indices()


def _peer(h2, x1, qh, k1, k2, g2, fw, uv, rows_per_batch, final):
    T = h2.shape[0]
    tm = LANES
    n = T // tm
    assert T % tm == 0 and tm == PEER_HEADS * TOPK_SLOTS and TOPK_SLOTS % PEER_BUFS == 0
    qspec = lambda f: pl.BlockSpec((PEER_HEADS, tm, PEER_KEY_DIM), f)
    key = pl.BlockSpec((PEER_KEYS, PEER_KEY_DIM // 2), lambda i: (0, 0))
    if rows_per_batch >= tm:
        assert rows_per_batch % tm == 0
        g2_spec = pl.BlockSpec((1, 1, D_MODEL), lambda i: (i * tm // rows_per_batch, 0, 0))
    else:
        assert tm % rows_per_batch == 0
        nb = tm // rows_per_batch
        g2_spec = pl.BlockSpec((nb, 1, D_MODEL), lambda i: (i, 0, 0))
    tok = pl.BlockSpec((tm, D_MODEL), lambda i: (i, 0))
    return pl.pallas_call(
        functools.partial(_peer_kernel, rows_per_batch=rows_per_batch, final=final),
        grid=(n,),
        in_specs=[tok, tok,
                  qspec(lambda i: (0, jnp.minimum(i + 1, n - 1), 0)),
                  qspec(lambda i: (0, 0, 0)),
                  key, key,
                  g2_spec,
                  pl.BlockSpec((1, D_MODEL), lambda i: (0, 0)),
                  pl.BlockSpec(memory_space=pl.ANY)],
        out_specs=tok,
        out_shape=jax.ShapeDtypeStruct((T, D_MODEL), f32),
        scratch_shapes=[pltpu.VMEM((PEER_SEL * UV_ROWS, LANES), jnp.uint32)] * PEER_BUFS
                       + [pltpu.SMEM((PEER_SEL, tm), jnp.int32),
                          pltpu.VMEM((PEER_SEL, tm), jnp.int32),
                          pltpu.VMEM((2, PEER_SEL, tm), f32),
                          pltpu.SemaphoreType.DMA((PEER_BUFS,)),
                          pltpu.SemaphoreType.DMA(())],
        compiler_params=_params(("arbitrary",)),
        name="peer",
    )(h2, x1, qh, qh, k1, k2, g2, fw, uv)


def _block_diag(w):
    n, d, _ = w.shape
    eye = jnp.eye(n, dtype=w.dtype)
    return (eye[:, None, :, None] * w[:, :, None, :]).reshape(n * d, n * d)


def _pad_lanes(v):
    return jnp.pad(v, (0, LANES - v.shape[0])).reshape(1, LANES)


def _prep_layer(l, p):
    w_in = p["w_in"][l]
    ws = [w_in[:, :Z_END], w_in[:, Z_END:XBC_END],
          jnp.pad(w_in[:, XBC_END:DT_END], ((0, 0), (0, LANES - SSD_HEADS))),
          w_in[:, DT_END:LX_END], w_in[:, LX_END:LY_END], w_in[:, LY_END:GS_END], w_in[:, GS_END:]]
    head = jnp.arange(SSD_INNER) // SSD_HEAD_DIM
    expand = (jnp.arange(LANES)[:, None] == head[None, :]).astype(f32)
    gmask = ((jnp.arange(LANES)[:, None] // SSD_STATE) == (head[None, :] // (SSD_HEADS // SSD_GROUPS))).astype(f32)
    bits = lambda t: lax.bitcast_convert_type(t.astype(bf16), jnp.uint16).astype(jnp.uint32)
    uv = (bits(p["peer_u"][l]) << 16) | bits(p["peer_v"][l])
    return dict(
        ws=[w.astype(bf16) for w in ws],
        ln1=p["ln1_w"][l].reshape(1, -1), ln2=p["ln2_w"][l].reshape(1, -1),
        ssd_cw=p["ssd_conv_w"][l], ssd_cb=p["ssd_conv_b"][l].reshape(1, -1),
        dtb=_pad_lanes(p["ssd_dt_bias"][l]), alog=_pad_lanes(p["ssd_a_log"][l]), dvec=_pad_lanes(p["ssd_d"][l]),
        expand=expand, gmask=gmask, nw=p["ssd_norm_w"][l].reshape(1, -1),
        lru_cw=p["lru_conv_w"][l], lru_cb=p["lru_conv_b"][l].reshape(1, -1),
        wa=_block_diag(p["lru_wa"][l]).astype(bf16), ba=p["lru_ba"][l].reshape(1, -1),
        wx=_block_diag(p["lru_wx"][l]).astype(bf16), bx=p["lru_bx"][l].reshape(1, -1),
        lam=p["lru_lambda"][l].reshape(1, -1),
        wbs=p["w_br_ssd"][l].astype(bf16), wbl=p["w_br_lru"][l].astype(bf16), wo=p["w_out"][l].astype(bf16),
        wq=p["peer_wq"][l].astype(bf16), k1=p["peer_k1"][l], k2=p["peer_k2"][l],
        uv=uv.reshape(PEER_EXPERTS * UV_ROWS, LANES),
    )


def _state_in(s):
    return s.transpose(0, 3, 1, 2).reshape(s.shape[0], SSD_STATE, SSD_INNER)


def _state_out(s):
    return s.reshape(s.shape[0], SSD_STATE, SSD_HEADS, SSD_HEAD_DIM).transpose(0, 2, 3, 1)


def _layer(x, mod, ssd_h0, ssd_buf, lru_h0, lru_buf, w, fw, final):
    B, L, _ = x.shape
    sh1, sc1, g1, sh2, sc2, g2 = [m.reshape(B, 1, D_MODEL) for m in jnp.split(mod, 6, axis=-1)]
    z, xbc, dt, lx, ly, gs, gl = _inproj(x, sc1, sh1, w["ln1"], w["ws"])
    y, st, sbuf = _ssd(xbc, dt, _state_in(ssd_h0), ssd_buf, w["ssd_cw"], w["ssd_cb"], w["dtb"], w["alog"],
                       w["dvec"], w["expand"], w["gmask"])
    hs, lh, lbuf = _lru(lx, lru_h0.reshape(B, 1, LRU_WIDTH), lru_buf, w["lru_cw"], w["lru_cb"], w["wa"], w["ba"],
                        w["wx"], w["bx"], w["lam"])
    x1, h2, qh = _merge(y, z, hs, ly, gs, gl, x, g1, sc2, sh2, w["nw"], w["ln2"], w["wbs"], w["wbl"], w["wo"],
                        w["wq"])
    T = B * L
    x2 = _peer(h2.reshape(T, D_MODEL), x1.reshape(T, D_MODEL), qh.reshape(PEER_HEADS, T, PEER_KEY_DIM),
               w["k1"], w["k2"], g2, fw, w["uv"], L, final)
    return x2.reshape(B, L, D_MODEL), _state_out(st), sbuf, lh.reshape(B, LRU_WIDTH), lbuf


def _trunk(x, c, ssd_h, ssd_buf, lru_h, lru_buf, p, layers):
    mod = _mod(c, p["w_mod"], p["b_mod"])
    fw = p["final_norm_w"].reshape(1, -1)
    outs = [[], [], [], []]
    for l in range(DEPTH):
        x, *new = _layer(x, mod[l], ssd_h[l], ssd_buf[l], lru_h[l], lru_buf[l], layers[l], fw, l == DEPTH - 1)
        for acc, v in zip(outs, new):
            acc.append(v)
    return (x,) + tuple(jnp.stack(o) for o in outs)


def kernel(x_prompt, x_sample, c_prompt, c_sample, state_ssd, state_ssd_conv, state_lru, state_lru_conv, ln1_w, ln2_w, w_mod, b_mod, w_in, ssd_conv_w, ssd_conv_b, ssd_dt_bias, ssd_a_log, ssd_d, ssd_norm_w, lru_conv_w, lru_conv_b, lru_wa, lru_ba, lru_wx, lru_bx, lru_lambda, w_br_ssd, w_br_lru, w_out, peer_wq, peer_k1, peer_k2, peer_u, peer_v, final_norm_w):
    p = dict(ln1_w=ln1_w, ln2_w=ln2_w, w_mod=w_mod, b_mod=b_mod, w_in=w_in, ssd_conv_w=ssd_conv_w,
             ssd_conv_b=ssd_conv_b, ssd_dt_bias=ssd_dt_bias, ssd_a_log=ssd_a_log, ssd_d=ssd_d,
             ssd_norm_w=ssd_norm_w, lru_conv_w=lru_conv_w, lru_conv_b=lru_conv_b, lru_wa=lru_wa, lru_ba=lru_ba,
             lru_wx=lru_wx, lru_bx=lru_bx, lru_lambda=lru_lambda, w_br_ssd=w_br_ssd, w_br_lru=w_br_lru,
             w_out=w_out, peer_wq=peer_wq, peer_k1=peer_k1, peer_k2=peer_k2, peer_u=peer_u, peer_v=peer_v,
             final_norm_w=final_norm_w)
    layers = [_prep_layer(l, p) for l in range(DEPTH)]
    bp = x_prompt.shape[0]
    zeros = lambda *s: jnp.zeros((DEPTH, bp) + s, f32)
    ys = _trunk(x_sample, c_sample, state_ssd, state_ssd_conv, state_lru, state_lru_conv, p, layers)
    yp = _trunk(x_prompt, c_prompt, zeros(SSD_HEADS, SSD_HEAD_DIM, SSD_STATE), zeros(CONV_W - 1, SSD_CONV_DIM),
                zeros(LRU_WIDTH), zeros(CONV_W - 1, LRU_WIDTH), p, layers)
    return (yp[0], ys[0]) + yp[1:] + ys[1:]
```
